```python
import math
import jax
import jax.numpy as jnp
from jax import lax
import numpy as np

D_MODEL = 2048
BATCH = 8
SEQ = 2048
DEPTH = 2

NORM_EPS = 1e-6
CONV_WIDTH = 4
SSD_D_INNER = D_MODEL
SSD_HEAD_DIM = 64
SSD_HEADS = SSD_D_INNER // SSD_HEAD_DIM
SSD_GROUPS = 4
SSD_STATE = 128
SSD_CHUNK = 128
SSD_XBC = SSD_D_INNER + 2 * SSD_GROUPS * SSD_STATE
MLSTM_HEADS = 8
MLSTM_V_DIM = D_MODEL // MLSTM_HEADS
MLSTM_QK_DIM = MLSTM_V_DIM // 2
MLSTM_CHUNK = 128
HGRN_EXPAND = 128
HGRN_HEADS = D_MODEL // HGRN_EXPAND
HGRN_V_DIM = D_MODEL // HGRN_HEADS
HGRN_CHUNK = 64
D_FF_DENSE = 5632
N_EXPERTS = 8
TOP_K = 2
D_FF_EXPERT = 7168
N_EVEN = (DEPTH + 1) // 2
N_ODD = DEPTH // 2
EVEN_SPLIT_SIZES = (SSD_D_INNER, SSD_XBC, SSD_HEADS,
                    MLSTM_HEADS * MLSTM_QK_DIM, MLSTM_HEADS * MLSTM_QK_DIM,
                    MLSTM_HEADS * MLSTM_V_DIM, MLSTM_HEADS * MLSTM_V_DIM,
                    MLSTM_HEADS, MLSTM_HEADS)
EVEN_IN = sum(EVEN_SPLIT_SIZES)
EVEN_MIX_OUT = SSD_D_INNER + MLSTM_HEADS * MLSTM_V_DIM
HGRN_IN = 4 * D_MODEL

kernel_name = 'hybrid_ssd_mlstm_hgrn2_moe'

F32 = jnp.float32


def rmsnorm(x, gain):
    xf = x.astype(F32)
    y = xf * lax.rsqrt(jnp.mean(xf * xf, axis=-1, keepdims=True) + NORM_EPS)
    return (y * gain.astype(F32)).astype(x.dtype)


def group_rmsnorm(y, gain, groups):
    shp = y.shape
    yf = y.astype(F32).reshape(*shp[:-1], groups, shp[-1] // groups)
    yf = yf * lax.rsqrt(jnp.mean(yf * yf, axis=-1, keepdims=True) + NORM_EPS)
    return yf.reshape(shp) * gain.astype(F32)


def causal_depthwise_conv(x, w, b):
    k = w.shape[0]
    y = lax.conv_general_dilated(x, w[:, None, :].astype(x.dtype), window_strides=(1,),
                                 padding=[(k - 1, 0)], dimension_numbers=('NWC', 'WIO', 'NWC'),
                                 feature_group_count=x.shape[-1])
    return y + b.astype(x.dtype)


def ssd_chunked(x, dt, a, bm, cm):
    bsz, s, nh, p = x.shape
    g, n = bm.shape[-2:]
    r = nh // g
    L = SSD_CHUNK
    nc = s // L
    xdt = (x.astype(F32) * dt[..., None]).reshape(bsz, nc, L, g, r, p)
    la = (dt * a).reshape(bsz, nc, L, g, r)
    bm = bm.astype(F32).reshape(bsz, nc, L, g, n)
    cm = cm.astype(F32).reshape(bsz, nc, L, g, n)
    acum = jnp.cumsum(la, axis=2)
    causal = jnp.tril(jnp.ones((L, L), dtype=bool))[None, None, :, :, None, None]
    decay = jnp.exp(jnp.where(causal, acum[:, :, :, None] - acum[:, :, None, :], -jnp.inf))
    cb = jnp.einsum('bclgn,bcsgn->bclsg', cm, bm)
    y_diag = jnp.einsum('bclsgr,bcsgrp->bclgrp', cb[..., None] * decay, xdt)
    decay_end = jnp.exp(acum[:, :, -1:] - acum)
    states = jnp.einsum('bclgn,bclgrp->bcgrpn', bm, xdt * decay_end[..., None])
    chunk_decay = jnp.exp(acum[:, :, -1])

    def step(carry, inp):
        st, dec = inp
        return carry * dec[..., None, None] + st, carry

    init = jnp.zeros((bsz, g, r, p, n), F32)
    _, prev = lax.scan(step, init, (jnp.moveaxis(states, 1, 0), jnp.moveaxis(chunk_decay, 1, 0)))
    prev = jnp.moveaxis(prev, 0, 1)
    y_off = jnp.einsum('bclgn,bcgrpn->bclgrp', cm, prev) * jnp.exp(acum)[..., None]
    return (y_diag + y_off).reshape(bsz, s, nh, p)


def mlstm_chunked(q, k, v, i_pre, f_pre):
    bsz, nh, s, dk = q.shape
    dv = v.shape[-1]
    L = MLSTM_CHUNK
    nc = s // L
    q = q.astype(F32).reshape(bsz, nh, nc, L, dk)
    k = (k.astype(F32) * dk ** -0.5).reshape(bsz, nh, nc, L, dk)
    v = v.astype(F32).reshape(bsz, nh, nc, L, dv)
    i_log = i_pre.astype(F32).reshape(bsz, nh, nc, L)
    f_log = jax.nn.log_sigmoid(f_pre.astype(F32)).reshape(bsz, nh, nc, L)
    b = jnp.cumsum(f_log, axis=-1)
    causal = jnp.tril(jnp.ones((L, L), dtype=bool))
    log_d = jnp.where(causal, b[..., :, None] - b[..., None, :] + i_log[..., None, :], -jnp.inf)
    m_intra = jnp.max(log_d, axis=-1)
    g_end = b[..., -1:] - b + i_log
    m_loc = jnp.max(g_end, axis=-1)
    kw = k * jnp.exp(g_end - m_loc[..., None])[..., None]
    c_loc = jnp.einsum('bhcld,bhcle->bhcde', kw, v)
    n_loc = jnp.sum(kw, axis=3)
    f_tot = b[..., -1]

    def step(carry, inp):
        c_st, n_st, m_st = carry
        c_l, n_l, m_l, f_c = inp
        m_new = jnp.maximum(f_c + m_st, m_l)
        a_prev = jnp.exp(f_c + m_st - m_new)
        a_loc = jnp.exp(m_l - m_new)
        c_new = c_st * a_prev[..., None, None] + c_l * a_loc[..., None, None]
        n_new = n_st * a_prev[..., None] + n_l * a_loc[..., None]
        return (c_new, n_new, m_new), (c_st, n_st, m_st)

    init = (jnp.zeros((bsz, nh, dk, dv), F32), jnp.zeros((bsz, nh, dk), F32),
            jnp.full((bsz, nh), -jnp.inf, F32))
    xs = (jnp.moveaxis(c_loc, 2, 0), jnp.moveaxis(n_loc, 2, 0),
          jnp.moveaxis(m_loc, 2, 0), jnp.moveaxis(f_tot, 2, 0))
    _, (c_prev, n_prev, m_prev) = lax.scan(step, init, xs)
    c_prev = jnp.moveaxis(c_prev, 0, 2)
    n_prev = jnp.moveaxis(n_prev, 0, 2)
    m_prev = jnp.moveaxis(m_prev, 0, 2)
    m_inter = b + m_prev[..., None]
    m_t = jnp.maximum(m_inter, m_intra)
    a_inter = jnp.exp(m_inter - m_t)
    pw = jnp.exp(log_d - m_t[..., None]) * jnp.einsum('bhcld,bhcsd->bhcls', q, k)
    num = jnp.einsum('bhcls,bhcse->bhcle', pw, v) + a_inter[..., None] * jnp.einsum('bhcld,bhcde->bhcle', q, c_prev)
    den = jnp.sum(pw, axis=-1) + a_inter * jnp.einsum('bhcld,bhcd->bhcl', q, n_prev)
    out = num / jnp.maximum(jnp.abs(den), jnp.exp(-m_t))[..., None]
    return out.reshape(bsz, nh, s, dv)


def hgrn2_chunked(q, k, v, f_log):
    bsz, nh, s, dk = q.shape
    dv = v.shape[-1]
    L = HGRN_CHUNK
    nc = s // L

    def to_chunks(t):
        return jnp.moveaxis(t.astype(F32).reshape(bsz, nh, nc, L, t.shape[-1]), 2, 0)

    causal = jnp.tril(jnp.ones((L, L), dtype=bool))[:, :, None]

    def step(state, inp):
        qc, kc, vc, gc = inp
        gcum = jnp.cumsum(gc, axis=2)
        dec = jnp.exp(jnp.where(causal, gcum[:, :, :, None, :] - gcum[:, :, None, :, :], -jnp.inf))
        att = jnp.einsum('bhtsk,bhsk->bhts', dec * qc[:, :, :, None, :], kc)
        o = jnp.einsum('bhts,bhsv->bhtv', att, vc) + jnp.einsum('bhtk,bhkv->bhtv', qc * jnp.exp(gcum), state)
        g_end = gcum[:, :, -1]
        k_end = kc * jnp.exp(g_end[:, :, None, :] - gcum)
        state = state * jnp.exp(g_end)[..., None] + jnp.einsum('bhsk,bhsv->bhkv', k_end, vc)
        return state, o

    init = jnp.zeros((bsz, nh, dk, dv), F32)
    _, o = lax.scan(step, init, (to_chunks(q), to_chunks(k), to_chunks(v), to_chunks(f_log)))
    return jnp.moveaxis(o, 0, 2).reshape(bsz, nh, s, dv)


def ssd_head_group(z, xbc, dt_raw, conv_w, conv_b, a_log, dt_bias, d_skip, norm_g):
    bsz, s, _ = z.shape
    xbc = jax.nn.silu(causal_depthwise_conv(xbc, conv_w, conv_b))
    xs, bm, cm = jnp.split(xbc, [SSD_D_INNER, SSD_D_INNER + SSD_GROUPS * SSD_STATE], axis=-1)
    xs = xs.reshape(bsz, s, SSD_HEADS, SSD_HEAD_DIM)
    bm = bm.reshape(bsz, s, SSD_GROUPS, SSD_STATE)
    cm = cm.reshape(bsz, s, SSD_GROUPS, SSD_STATE)
    dt = jax.nn.softplus(dt_raw.astype(F32) + dt_bias.astype(F32))
    a = -jnp.exp(a_log.astype(F32))
    y = ssd_chunked(xs, dt, a, bm, cm) + d_skip.astype(F32)[:, None] * xs.astype(F32)
    y = y.reshape(bsz, s, SSD_D_INNER) * jax.nn.silu(z.astype(F32))
    return group_rmsnorm(y, norm_g, SSD_GROUPS).astype(z.dtype)


def mlstm_head_group(q, k, v, o, i_pre, f_pre, conv_w, conv_b, b_i, b_f, norm_g):
    bsz, s, _ = q.shape
    qk = jax.nn.silu(causal_depthwise_conv(jnp.concatenate([q, k], axis=-1), conv_w, conv_b))
    q, k = jnp.split(qk, 2, axis=-1)

    def heads(t, d):
        return t.reshape(bsz, s, MLSTM_HEADS, d).transpose(0, 2, 1, 3)

    h = mlstm_chunked(heads(q, MLSTM_QK_DIM), heads(k, MLSTM_QK_DIM), heads(v, MLSTM_V_DIM),
                      (i_pre + b_i).transpose(0, 2, 1), (f_pre + b_f).transpose(0, 2, 1))
    h = h.transpose(0, 2, 1, 3).reshape(bsz, s, MLSTM_HEADS * MLSTM_V_DIM)
    h = jax.nn.sigmoid(o.astype(F32)) * h
    return group_rmsnorm(h, norm_g, MLSTM_HEADS).astype(q.dtype)


def even_mixer(xn, w_in, ssd_conv_w, ssd_conv_b, ssd_a_log, ssd_dt_bias, ssd_d, ssd_norm,
               mlstm_conv_w, mlstm_conv_b, mlstm_b_i, mlstm_b_f, mlstm_norm, w_out):
    proj = xn @ w_in
    cuts = np.cumsum(EVEN_SPLIT_SIZES)[:-1].tolist()
    z, xbc, dt_raw, q, k, v, o, i_pre, f_pre = jnp.split(proj, cuts, axis=-1)
    y_a = ssd_head_group(z, xbc, dt_raw, ssd_conv_w, ssd_conv_b, ssd_a_log, ssd_dt_bias, ssd_d, ssd_norm)
    y_b = mlstm_head_group(q, k, v, o, i_pre, f_pre, mlstm_conv_w, mlstm_conv_b, mlstm_b_i, mlstm_b_f, mlstm_norm)
    return jnp.concatenate([y_a, y_b], axis=-1) @ w_out


def hgrn2_mixer(xn, w_in, lb, norm_g, w_out):
    bsz, s, _ = xn.shape
    q, f_pre, i, g = jnp.split(xn @ w_in, 4, axis=-1)
    f = lb + (1.0 - lb) * jax.nn.sigmoid(f_pre.astype(F32))

    def heads(t):
        return t.reshape(bsz, s, HGRN_HEADS, -1).transpose(0, 2, 1, 3)

    o = hgrn2_chunked(heads(q), heads(1.0 - f), heads(i), heads(jnp.log(f)))
    o = o.transpose(0, 2, 1, 3).reshape(bsz, s, HGRN_HEADS * HGRN_V_DIM)
    o = group_rmsnorm(o, norm_g, HGRN_HEADS) * jax.nn.silu(g.astype(F32))
    return o.astype(xn.dtype) @ w_out


def swiglu(xn, w_gate, w_up, w_down):
    return (jax.nn.silu(xn @ w_gate) * (xn @ w_up)) @ w_down


def moe_swiglu(xn, w_router, w_gate, w_up, w_down):
    bsz, s, d = xn.shape
    xt = xn.reshape(-1, d)
    logits = (xt @ w_router).astype(F32)
    top_val, top_idx = lax.top_k(logits, TOP_K)
    gates = jax.nn.softmax(top_val, axis=-1)
    combine = jnp.sum(jax.nn.one_hot(top_idx, N_EXPERTS, dtype=F32) * gates[..., None], axis=1)
    out = jnp.zeros_like(xt)
    for e in range(N_EXPERTS):
        ye = swiglu(xt, w_gate[e], w_up[e], w_down[e])
        out = out + combine[:, e:e + 1].astype(xt.dtype) * ye
    return out.reshape(bsz, s, d)


def setup_inputs(seed: int = 0) -> dict:
    key = jax.random.key(seed)
    ks = iter(jax.random.split(key, 48))

    def nrm(shape, scale):
        return scale * jax.random.normal(next(ks), shape, F32)

    def gain(shape):
        return 1.0 + nrm(shape, 0.02)

    E, O, D = N_EVEN, N_ODD, D_MODEL
    x = nrm((BATCH, SEQ, D), 1.0)
    ev_norm_mix = gain((E, D))
    ev_w_in = nrm((E, D, EVEN_IN), D ** -0.5)
    ev_ssd_conv_w = nrm((E, CONV_WIDTH, SSD_XBC), CONV_WIDTH ** -0.5)
    ev_ssd_conv_b = nrm((E, SSD_XBC), 0.02)
    ev_ssd_a_log = jnp.log(jax.random.uniform(next(ks), (E, SSD_HEADS), F32, 1.0, 16.0))
    dt0 = jnp.exp(jax.random.uniform(next(ks), (E, SSD_HEADS), F32, math.log(1e-3), math.log(1e-1)))
    ev_ssd_dt_bias = dt0 + jnp.log(-jnp.expm1(-dt0))
    ev_ssd_d = 1.0 + nrm((E, SSD_HEADS), 0.1)
    ev_ssd_norm = gain((E, SSD_D_INNER))
    ev_mlstm_conv_w = nrm((E, CONV_WIDTH, 2 * MLSTM_HEADS * MLSTM_QK_DIM), CONV_WIDTH ** -0.5)
    ev_mlstm_conv_b = nrm((E, 2 * MLSTM_HEADS * MLSTM_QK_DIM), 0.02)
    ev_mlstm_b_i = nrm((E, MLSTM_HEADS), 0.1)
    ev_mlstm_b_f = jnp.linspace(3.0, 6.0, MLSTM_HEADS, dtype=F32)[None] + nrm((E, MLSTM_HEADS), 0.1)
    ev_mlstm_norm = gain((E, MLSTM_HEADS * MLSTM_V_DIM))
    ev_w_out = nrm((E, EVEN_MIX_OUT, D), EVEN_MIX_OUT ** -0.5)
    ev_norm_ffn = gain((E, D))
    ev_ffn_gate = nrm((E, D, D_FF_DENSE), D ** -0.5)
    ev_ffn_up = nrm((E, D, D_FF_DENSE), D ** -0.5)
    ev_ffn_down = nrm((E, D_FF_DENSE, D), D_FF_DENSE ** -0.5)
    od_norm_mix = gain((O, D))
    od_w_in = nrm((O, D, HGRN_IN), D ** -0.5)
    od_hgrn_norm = gain((O, HGRN_HEADS * HGRN_V_DIM))
    od_w_out = nrm((O, HGRN_HEADS * HGRN_V_DIM, D), (HGRN_HEADS * HGRN_V_DIM) ** -0.5)
    od_norm_ffn = gain((O, D))
    od_router = nrm((O, D, N_EXPERTS), D ** -0.5)
    od_exp_gate = nrm((O, N_EXPERTS, D, D_FF_EXPERT), D ** -0.5)
    od_exp_up = nrm((O, N_EXPERTS, D, D_FF_EXPERT), D ** -0.5)
    od_exp_down = nrm((O, N_EXPERTS, D_FF_EXPERT, D), D_FF_EXPERT ** -0.5)
    hgrn_lb_param = nrm((DEPTH, HGRN_HEADS * HGRN_EXPAND), 1.0)
    final_norm = gain((D,))
    return {'x': x, 'ev_norm_mix': ev_norm_mix, 'ev_w_in': ev_w_in,
            'ev_ssd_conv_w': ev_ssd_conv_w, 'ev_ssd_conv_b': ev_ssd_conv_b,
            'ev_ssd_a_log': ev_ssd_a_log, 'ev_ssd_dt_bias': ev_ssd_dt_bias, 'ev_ssd_d': ev_ssd_d,
            'ev_ssd_norm': ev_ssd_norm, 'ev_mlstm_conv_w': ev_mlstm_conv_w,
            'ev_mlstm_conv_b': ev_mlstm_conv_b, 'ev_mlstm_b_i': ev_mlstm_b_i,
            'ev_mlstm_b_f': ev_mlstm_b_f, 'ev_mlstm_norm': ev_mlstm_norm, 'ev_w_out': ev_w_out,
            'ev_norm_ffn': ev_norm_ffn, 'ev_ffn_gate': ev_ffn_gate, 'ev_ffn_up': ev_ffn_up,
            'ev_ffn_down': ev_ffn_down, 'od_norm_mix': od_norm_mix, 'od_w_in': od_w_in,
            'od_hgrn_norm': od_hgrn_norm, 'od_w_out': od_w_out, 'od_norm_ffn': od_norm_ffn,
            'od_router': od_router, 'od_exp_gate': od_exp_gate, 'od_exp_up': od_exp_up,
            'od_exp_down': od_exp_down, 'hgrn_lb_param': hgrn_lb_param, 'final_norm': final_norm}


def reference(x, ev_norm_mix, ev_w_in, ev_ssd_conv_w, ev_ssd_conv_b, ev_ssd_a_log, ev_ssd_dt_bias,
              ev_ssd_d, ev_ssd_norm, ev_mlstm_conv_w, ev_mlstm_conv_b, ev_mlstm_b_i, ev_mlstm_b_f,
              ev_mlstm_norm, ev_w_out, ev_norm_ffn, ev_ffn_gate, ev_ffn_up, ev_ffn_down,
              od_norm_mix, od_w_in, od_hgrn_norm, od_w_out, od_norm_ffn, od_router,
              od_exp_gate, od_exp_up, od_exp_down, hgrn_lb_param, final_norm):
    p_lb = jax.nn.softmax(hgrn_lb_param.astype(F32), axis=0)
    lb_all = jnp.cumsum(p_lb, axis=0) - p_lb[0]
    h = x
    for layer in range(DEPTH):
        j = layer // 2
        if layer % 2 == 0:
            h = h + even_mixer(rmsnorm(h, ev_norm_mix[j]), ev_w_in[j], ev_ssd_conv_w[j], ev_ssd_conv_b[j],
                               ev_ssd_a_log[j], ev_ssd_dt_bias[j], ev_ssd_d[j], ev_ssd_norm[j],
                               ev_mlstm_conv_w[j], ev_mlstm_conv_b[j], ev_mlstm_b_i[j], ev_mlstm_b_f[j],
                               ev_mlstm_norm[j], ev_w_out[j])
            h = h + swiglu(rmsnorm(h, ev_norm_ffn[j]), ev_ffn_gate[j], ev_ffn_up[j], ev_ffn_down[j])
        else:
            h = h + hgrn2_mixer(rmsnorm(h, od_norm_mix[j]), od_w_in[j], lb_all[layer],
                                od_hgrn_norm[j], od_w_out[j])
            h = h + moe_swiglu(rmsnorm(h, od_norm_ffn[j]), od_router[j], od_exp_gate[j],
                               od_exp_up[j], od_exp_down[j])
    return rmsnorm(h, final_norm)
```

```python
import functools
from typing import NamedTuple

import jax
import jax.numpy as jnp
from jax import lax
from jax.experimental import pallas as pl
from jax.experimental.pallas import tpu as pltpu

F32 = jnp.float32
BF16 = jnp.bfloat16

NORM_EPS = 1e-6
NEG_INF = float("-inf")

LANES = 128
SUBLANES = 8
VMEM_LIMIT_BYTES = 60 * 1024 * 1024


class Cfg(NamedTuple):
    d_model: int = 2048
    batch: int = 8
    seq: int = 2048
    conv_width: int = 4
    ssd_head_dim: int = 64
    ssd_group_heads: int = 8
    ssd_state: int = 128
    mlstm_qk: int = 128
    mlstm_v: int = 256
    hgrn_dk: int = 128
    d_ff_dense: int = 5632
    n_experts: int = 8
    top_k: int = 2
    d_ff_expert: int = 7168
    chunk: int = 128
    hgrn_chunk: int = 64
    hgrn_sub: int = 16
    tm: int = 1024
    tn_proj: int = 1024
    tn_ffn: int = 512
    tn_moe: int = 256
    rows_per_dma_step: int = 512

    @property
    def tokens(self):
        return self.batch * self.seq

    @property
    def ssd_heads(self):
        return self.d_model // self.ssd_head_dim

    @property
    def ssd_groups(self):
        return self.ssd_heads // self.ssd_group_heads

    @property
    def mlstm_heads(self):
        return self.d_model // self.mlstm_v

    @property
    def hgrn_heads(self):
        return self.d_model // self.hgrn_dk


def _cparams(sem):
    return pltpu.CompilerParams(dimension_semantics=sem, vmem_limit_bytes=VMEM_LIMIT_BYTES)


def _dot(a, b):
    return jnp.dot(a, b, preferred_element_type=F32)


def _dot_nt(a, b):
    return lax.dot_general(a, b, (((1,), (1,)), ((), ())), preferred_element_type=F32)


def _dot_tn(a, b):
    return lax.dot_general(a, b, (((0,), (0,)), ((), ())), preferred_element_type=F32)


def _sigmoid(x):
    return 1.0 / (1.0 + jnp.exp(-x))


def _silu(x):
    return x * _sigmoid(x)


def _softplus(x):
    return jnp.maximum(x, 0.0) + jnp.log(1.0 + jnp.exp(-jnp.abs(x)))


def _cumsum_rows(x):
    n = x.shape[0]
    row = lax.broadcasted_iota(jnp.int32, x.shape, 0)
    s = 1
    while s < n:
        x = x + jnp.where(row >= s, pltpu.roll(x, s, axis=0), 0.0)
        s *= 2
    return x


def _expand_heads(v, e):
    hi = v.astype(BF16)
    lo = (v - hi.astype(F32)).astype(BF16)
    return _dot(hi, e) + _dot(lo, e)


def _group_rmsnorm(y, gain_row, group):
    parts = []
    for j in range(y.shape[1] // group):
        yj = y[:, j * group:(j + 1) * group]
        ms = jnp.mean(yj * yj, axis=-1, keepdims=True)
        parts.append(yj * lax.rsqrt(ms + NORM_EPS))
    return jnp.concatenate(parts, axis=1) * gain_row


def _rmsnorm_rows_to(x_ref, gain_ref, out_ref, rows_per_step=128):
    n = x_ref.shape[0]
    step = min(rows_per_step, n)

    def body(i, carry):
        r = pl.multiple_of(i * step, step)
        x = x_ref[pl.ds(r, step), :].astype(F32)
        ms = jnp.mean(x * x, axis=-1, keepdims=True)
        out_ref[pl.ds(r, step), :] = (x * lax.rsqrt(ms + NORM_EPS) * gain_ref[...]).astype(out_ref.dtype)
        return carry

    lax.fori_loop(0, n // step, body, 0)


def _causal_conv(x, tail_ref, w_ref, b_ref):
    n = x.shape[0]
    width = w_ref.shape[0]
    tail = tail_ref[...]
    row8 = lax.broadcasted_iota(jnp.int32, tail.shape, 0)
    acc = x * w_ref[width - 1:width, :] + b_ref[...]
    for k in range(1, width):
        rolled = pltpu.roll(x, k, axis=0)
        head = jnp.where(row8 < k, pltpu.roll(tail, k, axis=0), rolled[:SUBLANES])
        shifted = jnp.concatenate([head, rolled[SUBLANES:]], axis=0)
        acc = acc + shifted * w_ref[width - 1 - k:width - k, :]
    tail_ref[...] = x[n - SUBLANES:]
    return acc


def _norm_matmul_kernel(x_ref, g_ref, w_ref, o_ref, xn_ref):
    @pl.when(pl.program_id(1) == 0)
    def _():
        _rmsnorm_rows_to(x_ref, g_ref, xn_ref)

    o_ref[...] = _dot(xn_ref[...], w_ref[...].astype(BF16)).astype(o_ref.dtype)


def norm_matmul(x, gain, w, out_dtype, tm, tn):
    t, d = x.shape
    n = w.shape[1]
    return pl.pallas_call(
        _norm_matmul_kernel,
        out_shape=jax.ShapeDtypeStruct((t, n), out_dtype),
        grid=(t // tm, n // tn),
        in_specs=[pl.BlockSpec((tm, d), lambda i, j: (i, 0)),
                  pl.BlockSpec((1, d), lambda i, j: (0, 0)),
                  pl.BlockSpec((d, tn), lambda i, j: (0, j))],
        out_specs=pl.BlockSpec((tm, tn), lambda i, j: (i, j)),
        scratch_shapes=[pltpu.VMEM((tm, d), BF16)],
        compiler_params=_cparams(("parallel", "arbitrary")),
        name="norm_matmul",
    )(x, gain.reshape(1, d), w)


def _matmul_res_kernel(*refs, n_in):
    a_refs = refs[:n_in]
    w_refs = refs[n_in:2 * n_in]
    res_ref = refs[2 * n_in]
    o_ref = refs[2 * n_in + 1]
    acc = res_ref[...]
    for a_ref, w_ref in zip(a_refs, w_refs):
        acc = acc + _dot(a_ref[...], w_ref[...].astype(BF16))
    o_ref[...] = acc


def matmul_res(a_list, w, res, tm, tn):
    n_in = len(a_list)
    t, k = a_list[0].shape
    n = w.shape[1]
    in_specs = [pl.BlockSpec((tm, k), lambda i, j: (i, 0)) for _ in range(n_in)]
    in_specs += [pl.BlockSpec((k, tn), functools.partial(lambda i, j, s: (s, j), s=s)) for s in range(n_in)]
    in_specs += [pl.BlockSpec((tm, tn), lambda i, j: (i, j))]
    return pl.pallas_call(
        functools.partial(_matmul_res_kernel, n_in=n_in),
        out_shape=jax.ShapeDtypeStruct((t, n), F32),
        grid=(t // tm, n // tn),
        in_specs=in_specs,
        out_specs=pl.BlockSpec((tm, tn), lambda i, j: (i, j)),
        compiler_params=_cparams(("parallel", "arbitrary")),
        name="matmul_res",
    )(*a_list, *([w] * n_in), res)


def _ssd_kernel(z_ref, xs_ref, bc_ref, dt_ref, cwx_ref, cbx_ref, cwb_ref, cbb_ref, alog_ref,
                dtb_ref, dskip_ref, ng_ref, e_ref, o_ref, state_ref, tailx_ref, tailb_ref,
                *, n_heads, group_heads, head_dim, n_state):
    @pl.when(pl.program_id(1) == 0)
    def _():
        state_ref[...] = jnp.zeros_like(state_ref)
        tailx_ref[...] = jnp.zeros_like(tailx_ref)
        tailb_ref[...] = jnp.zeros_like(tailb_ref)

    rows = xs_ref.shape[0]
    n_groups = n_heads // group_heads
    gw = group_heads * head_dim
    xs = _silu(_causal_conv(xs_ref[...].astype(F32), tailx_ref, cwx_ref, cbx_ref))
    bc = _silu(_causal_conv(bc_ref[...].astype(F32), tailb_ref, cwb_ref, cbb_ref))

    lane = lax.broadcasted_iota(jnp.int32, (rows, LANES), 1)
    dt = jnp.where(lane < n_heads, _softplus(dt_ref[...] + dtb_ref[...]), 0.0)
    la = dt * (-jnp.exp(alog_ref[...]))
    acum = _cumsum_rows(la)
    acum_t = acum.T
    a_last = acum[rows - 1:rows, :]
    e = e_ref[...]
    dt_x = _expand_heads(dt, e)
    eacum_x = _expand_heads(jnp.exp(acum), e)
    dend_x = _expand_heads(jnp.exp(a_last - acum), e)
    cdec_x = eacum_x[rows - 1:rows, :]

    xdt = xs * dt_x
    xdt_b = xdt.astype(BF16)
    xdend_b = (xdt * dend_x).astype(BF16)
    state = state_ref[...]
    state_b = state.astype(BF16)

    r_i = lax.broadcasted_iota(jnp.int32, (rows, rows), 0)
    c_i = lax.broadcasted_iota(jnp.int32, (rows, rows), 1)
    causal = r_i >= c_i
    lane_lo = lane < (LANES // 2)

    y_diag, y_off, st_new = [], [], []
    heads_per_tile = LANES // head_dim
    for g in range(n_groups):
        b_g = bc[:, g * n_state:(g + 1) * n_state].astype(BF16)
        c_g = bc[:, (n_groups + g) * n_state:(n_groups + g + 1) * n_state].astype(BF16)
        cb = _dot_nt(c_g, b_g)
        y_off.append(_dot(c_g, state_b[:, g * gw:(g + 1) * gw]))
        st_new.append(_dot_tn(b_g, xdend_b[:, g * gw:(g + 1) * gw]))
        for p in range(group_heads // heads_per_tile):
            h0 = g * group_heads + p * heads_per_tile
            ms = []
            for h in range(h0, h0 + heads_per_tile):
                seg = acum[:, h:h + 1] - acum_t[h:h + 1, :]
                dec = jnp.exp(jnp.where(causal, seg, NEG_INF))
                ms.append((cb * dec).astype(BF16))
            lhs = jnp.concatenate(ms, axis=1)
            xp = xdt_b[:, h0 * head_dim:h0 * head_dim + LANES]
            zero = jnp.zeros_like(xp)
            rhs = jnp.concatenate([jnp.where(lane_lo, xp, zero), jnp.where(lane_lo, zero, xp)], axis=0)
            y_diag.append(_dot(lhs, rhs))
    y = (jnp.concatenate(y_diag, axis=1) + jnp.concatenate(y_off, axis=1) * eacum_x
         + dskip_ref[...] * xs)
    state_ref[...] = state * cdec_x + jnp.concatenate(st_new, axis=1)
    y = y * _silu(z_ref[...].astype(F32))
    o_ref[...] = _group_rmsnorm(y, ng_ref[...], gw).astype(o_ref.dtype)


def ssd_mixer(cfg, proj, gates, cwx, cbx, cwb, cbb, a_log_p, dt_bias_p, d_skip_x, norm_g, e_mat):
    d, L = cfg.d_model, cfg.chunk
    nc = cfg.seq // L
    bcw = 2 * cfg.ssd_groups * cfg.ssd_state
    row = lambda b, c: b * nc + c
    full = lambda shape: pl.BlockSpec(shape, lambda b, c: (0, 0))
    kern = functools.partial(_ssd_kernel, n_heads=cfg.ssd_heads, group_heads=cfg.ssd_group_heads,
                             head_dim=cfg.ssd_head_dim, n_state=cfg.ssd_state)
    return pl.pallas_call(
        kern,
        out_shape=jax.ShapeDtypeStruct((cfg.tokens, d), BF16),
        grid=(cfg.batch, nc),
        in_specs=[pl.BlockSpec((L, d), lambda b, c: (row(b, c), 0)),
                  pl.BlockSpec((L, d), lambda b, c: (row(b, c), 1)),
                  pl.BlockSpec((L, bcw), lambda b, c: (row(b, c), 4 * d // bcw)),
                  pl.BlockSpec((L, LANES), lambda b, c: (row(b, c), 0)),
                  full(cwx.shape), full(cbx.shape), full(cwb.shape), full(cbb.shape),
                  full(a_log_p.shape), full(dt_bias_p.shape), full(d_skip_x.shape),
                  full(norm_g.shape), full(e_mat.shape)],
        out_specs=pl.BlockSpec((L, d), lambda b, c: (row(b, c), 0)),
        scratch_shapes=[pltpu.VMEM((cfg.ssd_state, d), F32),
                        pltpu.VMEM((SUBLANES, d), F32),
                        pltpu.VMEM((SUBLANES, bcw), F32)],
        compiler_params=_cparams(("parallel", "arbitrary")),
        name="ssd_mixer",
    )(proj, proj, proj, gates, cwx, cbx, cwb, cbb, a_log_p, dt_bias_p, d_skip_x, norm_g, e_mat)


def _mlstm_kernel(q_ref, k_ref, v_ref, og_ref, ig_ref, fg_ref, cwq_ref, cbq_ref, cwk_ref, cbk_ref,
                  bi_ref, bf_ref, ng_ref, o_ref, c_ref, n_ref, m_ref, tailq_ref, tailk_ref,
                  *, n_heads, dk, dv):
    @pl.when(pl.program_id(1) == 0)
    def _():
        c_ref[...] = jnp.zeros_like(c_ref)
        n_ref[...] = jnp.zeros_like(n_ref)
        m_ref[...] = jnp.full(m_ref.shape, NEG_INF, F32)
        tailq_ref[...] = jnp.zeros_like(tailq_ref)
        tailk_ref[...] = jnp.zeros_like(tailk_ref)

    rows = q_ref.shape[0]
    q = _silu(_causal_conv(q_ref[...].astype(F32), tailq_ref, cwq_ref, cbq_ref))
    k = _silu(_causal_conv(k_ref[...].astype(F32), tailk_ref, cwk_ref, cbk_ref)) * (dk ** -0.5)
    i_log = ig_ref[...] + bi_ref[...]
    f_log = -_softplus(-(fg_ref[...] + bf_ref[...]))
    b = _cumsum_rows(f_log)
    b_t = b.T
    i_t = i_log.T
    g_end = b[rows - 1:rows, :] - b + i_log
    r_i = lax.broadcasted_iota(jnp.int32, (rows, rows), 0)
    c_i = lax.broadcasted_iota(jnp.int32, (rows, rows), 1)
    causal = r_i >= c_i
    rep = dv // LANES

    def wide(x):
        return jnp.concatenate([x] * rep, axis=1)

    outs = []
    for h in range(n_heads):
        qh = q[:, h * dk:(h + 1) * dk]
        kh = k[:, h * dk:(h + 1) * dk]
        vh = v_ref[:, h * dv:(h + 1) * dv]
        qb = qh.astype(BF16)
        b_col = jnp.broadcast_to(b[:, h:h + 1], (rows, LANES))
        log_d = jnp.where(causal, b_col - b_t[h:h + 1, :] + i_t[h:h + 1, :], NEG_INF)
        m_intra = jnp.max(log_d, axis=-1, keepdims=True)
        m_prev = m_ref[h:h + 1, :]
        m_inter = b_col + m_prev
        m_t = jnp.maximum(m_inter, m_intra)
        a_inter = jnp.exp(m_inter - m_t)
        pw = jnp.exp(log_d - m_t) * _dot_nt(qb, kh.astype(BF16))
        c_prev = c_ref[h]
        n_prev = n_ref[h:h + 1, :]
        num = _dot(pw.astype(BF16), vh) + wide(a_inter) * _dot(qb, c_prev.astype(BF16))
        den = (jnp.sum(pw, axis=-1, keepdims=True)
               + a_inter * jnp.sum(qh * n_prev, axis=-1, keepdims=True))
        outs.append(num / wide(jnp.maximum(jnp.abs(den), jnp.exp(-m_t))))
        ge = jnp.broadcast_to(g_end[:, h:h + 1], (rows, LANES))
        m_loc = jnp.max(ge, axis=0, keepdims=True)
        kw = kh * jnp.exp(ge - m_loc)
        f_tot = b_col[rows - 1:rows, :]
        m_new = jnp.maximum(f_tot + m_prev, m_loc)
        a_prev = jnp.exp(f_tot + m_prev - m_new)
        a_loc = jnp.exp(m_loc - m_new)
        c_ref[h] = c_prev * wide(a_prev) + _dot_tn(kw.astype(BF16), vh) * wide(a_loc)
        n_ref[h:h + 1, :] = n_prev * a_prev + jnp.sum(kw, axis=0, keepdims=True) * a_loc
        m_ref[h:h + 1, :] = m_new
    hcat = jnp.concatenate(outs, axis=1) * _sigmoid(og_ref[...].astype(F32))
    o_ref[...] = _group_rmsnorm(hcat, ng_ref[...], dv).astype(o_ref.dtype)


def mlstm_mixer(cfg, proj, gates, cwq, cbq, cwk, cbk, b_i_p, b_f_p, norm_g):
    d, L = cfg.d_model, cfg.chunk
    nc = cfg.seq // L
    qw = cfg.mlstm_heads * cfg.mlstm_qk
    row = lambda b, c: b * nc + c
    full = lambda shape: pl.BlockSpec(shape, lambda b, c: (0, 0))
    kern = functools.partial(_mlstm_kernel, n_heads=cfg.mlstm_heads, dk=cfg.mlstm_qk, dv=cfg.mlstm_v)
    q_blk = (4 * d + 2 * cfg.ssd_groups * cfg.ssd_state) // qw
    return pl.pallas_call(
        kern,
        out_shape=jax.ShapeDtypeStruct((cfg.tokens, d), BF16),
        grid=(cfg.batch, nc),
        in_specs=[pl.BlockSpec((L, qw), lambda b, c: (row(b, c), q_blk)),
                  pl.BlockSpec((L, qw), lambda b, c: (row(b, c), q_blk + 1)),
                  pl.BlockSpec((L, d), lambda b, c: (row(b, c), 2)),
                  pl.BlockSpec((L, d), lambda b, c: (row(b, c), 3)),
                  pl.BlockSpec((L, LANES), lambda b, c: (row(b, c), 1)),
                  pl.BlockSpec((L, LANES), lambda b, c: (row(b, c), 2)),
                  full(cwq.shape), full(cbq.shape), full(cwk.shape), full(cbk.shape),
                  full(b_i_p.shape), full(b_f_p.shape), full(norm_g.shape)],
        out_specs=pl.BlockSpec((L, d), lambda b, c: (row(b, c), 0)),
        scratch_shapes=[pltpu.VMEM((cfg.mlstm_heads, cfg.mlstm_qk, cfg.mlstm_v), F32),
                        pltpu.VMEM((cfg.mlstm_heads, cfg.mlstm_qk), F32),
                        pltpu.VMEM((cfg.mlstm_heads, LANES), F32),
                        pltpu.VMEM((SUBLANES, qw), F32),
                        pltpu.VMEM((SUBLANES, qw), F32)],
        compiler_params=_cparams(("parallel", "arbitrary")),
        name="mlstm_mixer",
    )(proj, proj, proj, proj, gates, gates, cwq, cbq, cwk, cbk, b_i_p, b_f_p, norm_g)


def _hgrn_kernel(q_ref, f_ref, v_ref, g_ref, lbp_ref, ng_ref, o_ref,
                 st_ref, q_s, k_s, gc_s, v_s, o_s, *, n_heads, dk, sub):
    @pl.when(pl.program_id(1) == 0)
    def _():
        st_ref[...] = jnp.zeros_like(st_ref)

    rows = q_ref.shape[0]
    a0 = lbp_ref[0:1, :]
    a1 = lbp_ref[1:2, :]
    amax = jnp.maximum(a0, a1)
    e0 = jnp.exp(a0 - amax)
    e1 = jnp.exp(a1 - amax)
    p0 = e0 / (e0 + e1)
    p1 = e1 / (e0 + e1)
    lb = (p0 + p1) - p0
    f = lb + (1.0 - lb) * _sigmoid(f_ref[...])
    k_s[...] = 1.0 - f
    gc_s[...] = _cumsum_rows(jnp.log(f))
    q_s[...] = q_ref[...].astype(F32)
    v_s[...] = v_ref[...].astype(F32)
    row_sub = lax.broadcasted_iota(jnp.int32, (sub, dk), 0)

    def head(h, carry):
        off = pl.multiple_of(h * dk, dk)
        q = q_s[:, pl.ds(off, dk)]
        k = k_s[:, pl.ds(off, dk)]
        gc = gc_s[:, pl.ds(off, dk)]
        v = v_s[:, pl.ds(off, dk)]
        v_b = v.astype(BF16)
        st = st_ref[h]
        g_end = gc[rows - 1:rows, :]
        o = _dot_nt((q * jnp.exp(gc)).astype(BF16), st.astype(BF16))
        k_end = (k * jnp.exp(g_end - gc)).astype(BF16)
        st_ref[h] = st * jnp.exp(g_end) + _dot_tn(v_b, k_end)
        blocks = []
        for i in range(rows // sub):
            r0 = i * sub
            qi = q[r0:r0 + sub]
            gi = gc[r0:r0 + sub]
            ki = k[r0:r0 + sub]
            vi = v[r0:r0 + sub]
            oi = o[r0:r0 + sub]
            if i > 0:
                gref = gc[r0 - 1:r0, :]
                qt = (qi * jnp.exp(gi - gref)).astype(BF16)
                kt = (k[:r0] * jnp.exp(gref - gc[:r0])).astype(BF16)
                att = _dot_nt(qt, kt)
                oi = oi + _dot(att.astype(BF16), v_b[:r0])
            for s in range(sub):
                dec = jnp.exp(jnp.where(row_sub >= s, gi - gi[s:s + 1, :], NEG_INF))
                att_s = jnp.sum(qi * dec * ki[s:s + 1, :], axis=-1, keepdims=True)
                oi = oi + att_s * vi[s:s + 1, :]
            blocks.append(oi)
        o_s[:, pl.ds(off, dk)] = jnp.concatenate(blocks, axis=0)
        return carry

    lax.fori_loop(0, n_heads, head, 0)
    y = _group_rmsnorm(o_s[...], ng_ref[...], dk) * _silu(g_ref[...].astype(F32))
    o_ref[...] = y.astype(o_ref.dtype)


def hgrn_mixer(cfg, proj, f_pre, lb_param, norm_g):
    d, L = cfg.d_model, cfg.hgrn_chunk
    nc = cfg.seq // L
    row = lambda b, c: b * nc + c
    full = lambda shape: pl.BlockSpec(shape, lambda b, c: (0, 0))
    kern = functools.partial(_hgrn_kernel, n_heads=cfg.hgrn_heads, dk=cfg.hgrn_dk, sub=cfg.hgrn_sub)
    return pl.pallas_call(
        kern,
        out_shape=jax.ShapeDtypeStruct((cfg.tokens, d), BF16),
        grid=(cfg.batch, nc),
        in_specs=[pl.BlockSpec((L, d), lambda b, c: (row(b, c), 0)),
                  pl.BlockSpec((L, d), lambda b, c: (row(b, c), 0)),
                  pl.BlockSpec((L, d), lambda b, c: (row(b, c), 1)),
                  pl.BlockSpec((L, d), lambda b, c: (row(b, c), 2)),
                  full(lb_param.shape), full(norm_g.shape)],
        out_specs=pl.BlockSpec((L, d), lambda b, c: (row(b, c), 0)),
        scratch_shapes=[pltpu.VMEM((cfg.hgrn_heads, cfg.hgrn_dk, cfg.hgrn_dk), F32)]
                       + [pltpu.VMEM((L, d), F32)] * 5,
        compiler_params=_cparams(("parallel", "arbitrary")),
        name="hgrn_mixer",
    )(proj, f_pre, proj, proj, lb_param, norm_g)


def _ffn_kernel(te_ref, na_ref, x_ref, gain_ref, wg_ref, wu_ref, wd_ref, o_ref, xn_ref, *, dense):
    i = pl.program_id(0)
    j = pl.program_id(1)
    active = i < na_ref[0]

    @pl.when(jnp.logical_and(j == 0, active))
    def _():
        if dense:
            _rmsnorm_rows_to(x_ref, gain_ref, xn_ref)
            o_ref[...] = x_ref[...]
        else:
            xn_ref[...] = x_ref[...].astype(BF16)
            o_ref[...] = jnp.zeros_like(o_ref)

    @pl.when(jnp.logical_and(j == 0, jnp.logical_not(active)))
    def _():
        o_ref[...] = jnp.zeros_like(o_ref)

    @pl.when(active)
    def _():
        xn = xn_ref[...]
        gate = _dot(xn, wg_ref[0].astype(BF16))
        up = _dot(xn, wu_ref[0].astype(BF16))
        act = (_silu(gate) * up).astype(BF16)
        o_ref[...] += _dot(act, wd_ref[0].astype(BF16))


def swiglu_tiles(x, gain, w_gate, w_up, w_down, tile_expert, n_active, tm, tn, dense):
    r, d = x.shape
    f = w_gate.shape[2]
    grid_spec = pltpu.PrefetchScalarGridSpec(
        num_scalar_prefetch=2,
        grid=(r // tm, f // tn),
        in_specs=[pl.BlockSpec((tm, d), lambda i, j, te, na: (i, 0)),
                  pl.BlockSpec((1, d), lambda i, j, te, na: (0, 0)),
                  pl.BlockSpec((1, d, tn), lambda i, j, te, na: (te[i], 0, j)),
                  pl.BlockSpec((1, d, tn), lambda i, j, te, na: (te[i], 0, j)),
                  pl.BlockSpec((1, tn, d), lambda i, j, te, na: (te[i], j, 0))],
        out_specs=pl.BlockSpec((tm, d), lambda i, j, te, na: (i, 0)),
        scratch_shapes=[pltpu.VMEM((tm, d), BF16)],
    )
    return pl.pallas_call(
        functools.partial(_ffn_kernel, dense=dense),
        out_shape=jax.ShapeDtypeStruct((r, d), F32),
        grid_spec=grid_spec,
        compiler_params=_cparams(("parallel", "arbitrary")),
        name="swiglu_dense" if dense else "swiglu_moe",
    )(tile_expert, n_active, x, gain.reshape(1, d), w_gate, w_up, w_down)


def _router_kernel(x_ref, gain_ref, wr_ref, xn_ref, route_ref, count_ref, run_ref, tri_ref, *, n_experts):
    i = pl.program_id(0)
    rows = x_ref.shape[0]

    @pl.when(i == 0)
    def _():
        run_ref[...] = jnp.zeros_like(run_ref)
        r_i = lax.broadcasted_iota(jnp.int32, (rows, rows), 0)
        c_i = lax.broadcasted_iota(jnp.int32, (rows, rows), 1)
        tri_ref[...] = jnp.where(r_i > c_i, 1.0, 0.0).astype(BF16)

    _rmsnorm_rows_to(x_ref, gain_ref, xn_ref)
    xn = xn_ref[...]
    x_hi = xn.astype(BF16)
    x_lo = (xn - x_hi.astype(F32)).astype(BF16)
    w = wr_ref[...]
    w_hi = w.astype(BF16)
    w_lo = (w - w_hi.astype(F32)).astype(BF16)
    logits = _dot(x_hi, w_hi) + (_dot(x_lo, w_hi) + _dot(x_hi, w_lo))

    lane = lax.broadcasted_iota(jnp.int32, logits.shape, 1)
    lane_f = lane.astype(F32)
    big = float(LANES)
    l1 = jnp.where(lane < n_experts, logits, NEG_INF)
    m1 = jnp.max(l1, axis=-1, keepdims=True)
    idx1 = jnp.min(jnp.where(l1 == m1, lane_f, big), axis=-1, keepdims=True)
    hot1 = lane_f == idx1
    l2 = jnp.where(hot1, NEG_INF, l1)
    m2 = jnp.max(l2, axis=-1, keepdims=True)
    idx2 = jnp.min(jnp.where(l2 == m2, lane_f, big), axis=-1, keepdims=True)
    hot2 = lane_f == idx2
    e2 = jnp.exp(m2 - m1)
    g1 = 1.0 / (1.0 + e2)
    g2 = e2 / (1.0 + e2)

    cnt = jnp.where(hot1, 1.0, 0.0) + jnp.where(hot2, 1.0, 0.0)
    before = _dot(tri_ref[...], cnt.astype(BF16)) + run_ref[...]
    rank1 = jnp.sum(jnp.where(hot1, before, 0.0), axis=-1, keepdims=True)
    rank2 = jnp.sum(jnp.where(hot2, before, 0.0), axis=-1, keepdims=True)
    run_ref[...] = run_ref[...] + jnp.sum(cnt, axis=0, keepdims=True)
    count_ref[...] = jnp.broadcast_to(run_ref[...], count_ref.shape)

    out = jnp.zeros(logits.shape, F32)
    for pos, val in enumerate((idx1, idx2, rank1, rank2, g1, g2)):
        out = jnp.where(lane == pos, val, out)
    route_ref[...] = out


def router(x, gain, w_router_p, n_experts, tm):
    t, d = x.shape
    return pl.pallas_call(
        functools.partial(_router_kernel, n_experts=n_experts),
        out_shape=(jax.ShapeDtypeStruct((t, d), F32),
                   jax.ShapeDtypeStruct((t, LANES), F32),
                   jax.ShapeDtypeStruct((SUBLANES, LANES), F32)),
        grid=(t // tm,),
        in_specs=[pl.BlockSpec((tm, d), lambda i: (i, 0)),
                  pl.BlockSpec((1, d), lambda i: (0, 0)),
                  pl.BlockSpec((d, LANES), lambda i: (0, 0))],
        out_specs=(pl.BlockSpec((tm, d), lambda i: (i, 0)),
                   pl.BlockSpec((tm, LANES), lambda i: (i, 0)),
                   pl.BlockSpec((SUBLANES, LANES), lambda i: (0, 0))),
        scratch_shapes=[pltpu.VMEM((1, LANES), F32), pltpu.VMEM((tm, tm), BF16)],
        compiler_params=_cparams(("arbitrary",)),
        name="router",
    )(x, gain.reshape(1, d), w_router_p)


def _row_copy(src_hbm, dst_hbm, src_row, dst_row, dst_col, width, sem):
    return pltpu.make_async_copy(src_hbm.at[pl.ds(src_row, 1), pl.ds(0, width)],
                                 dst_hbm.at[pl.ds(dst_row, 1), pl.ds(dst_col, width)], sem)


def _scatter_rows_kernel(dest_ref, src_hbm, init_hbm, dst_hbm, sem, *, rows_per_step, fanout):
    del init_hbm
    base = pl.program_id(0) * rows_per_step
    width = src_hbm.shape[1]

    def start(r, carry):
        for s in range(fanout):
            _row_copy(src_hbm, dst_hbm, base + r, dest_ref[(base + r) * fanout + s], 0, width, sem).start()
        return carry

    def wait(r, carry):
        for s in range(fanout):
            _row_copy(src_hbm, dst_hbm, 0, 0, 0, width, sem).wait()
        return carry

    lax.fori_loop(0, rows_per_step, start, 0)
    lax.fori_loop(0, rows_per_step, wait, 0)


def scatter_rows(src, dest, n_rows, rows_per_step, fanout):
    t, w = src.shape
    grid_spec = pltpu.PrefetchScalarGridSpec(
        num_scalar_prefetch=1,
        grid=(t // rows_per_step,),
        in_specs=[pl.BlockSpec(memory_space=pl.ANY), pl.BlockSpec(memory_space=pl.ANY)],
        out_specs=pl.BlockSpec(memory_space=pl.ANY),
        scratch_shapes=[pltpu.SemaphoreType.DMA(())],
    )
    return pl.pallas_call(
        functools.partial(_scatter_rows_kernel, rows_per_step=rows_per_step, fanout=fanout),
        out_shape=jax.ShapeDtypeStruct((n_rows, w), src.dtype),
        grid_spec=grid_spec,
        input_output_aliases={2: 0},
        compiler_params=pltpu.CompilerParams(dimension_semantics=("arbitrary",), has_side_effects=True),
        name="scatter_rows",
    )(dest, src, jnp.zeros((n_rows, w), src.dtype))


def _gather_rows_kernel(idx_ref, src_hbm, dst_hbm, sem, *, rows_per_step, fanout):
    base = pl.program_id(0) * rows_per_step
    width = src_hbm.shape[1]

    def start(r, carry):
        for s in range(fanout):
            _row_copy(src_hbm, dst_hbm, idx_ref[(base + r) * fanout + s], base + r, s * width, width, sem).start()
        return carry

    def wait(r, carry):
        for s in range(fanout):
            _row_copy(src_hbm, dst_hbm, 0, 0, 0, width, sem).wait()
        return carry

    lax.fori_loop(0, rows_per_step, start, 0)
    lax.fori_loop(0, rows_per_step, wait, 0)


def gather_rows(src, idx, n_tokens, rows_per_step, fanout):
    w = src.shape[1]
    grid_spec = pltpu.PrefetchScalarGridSpec(
        num_scalar_prefetch=1,
        grid=(n_tokens // rows_per_step,),
        in_specs=[pl.BlockSpec(memory_space=pl.ANY)],
        out_specs=pl.BlockSpec(memory_space=pl.ANY),
        scratch_shapes=[pltpu.SemaphoreType.DMA(())],
    )
    return pl.pallas_call(
        functools.partial(_gather_rows_kernel, rows_per_step=rows_per_step, fanout=fanout),
        out_shape=jax.ShapeDtypeStruct((n_tokens, fanout * w), src.dtype),
        grid_spec=grid_spec,
        compiler_params=pltpu.CompilerParams(dimension_semantics=("arbitrary",), has_side_effects=True),
        name="gather_rows",
    )(idx, src)


def _combine_kernel(h_ref, y_ref, route_ref, gain_ref, o_ref):
    d = h_ref.shape[1]
    route = route_ref[...]
    acc = h_ref[...] + route[:, 4:5] * y_ref[:, 0:d] + route[:, 5:6] * y_ref[:, d:2 * d]
    ms = jnp.mean(acc * acc, axis=-1, keepdims=True)
    o_ref[...] = acc * lax.rsqrt(ms + NORM_EPS) * gain_ref[...]


def combine_norm(h, y_pairs, route, gain, tm):
    t, d = h.shape
    return pl.pallas_call(
        _combine_kernel,
        out_shape=jax.ShapeDtypeStruct((t, d), F32),
        grid=(t // tm,),
        in_specs=[pl.BlockSpec((tm, d), lambda i: (i, 0)),
                  pl.BlockSpec((tm, 2 * d), lambda i: (i, 0)),
                  pl.BlockSpec((tm, LANES), lambda i: (i, 0)),
                  pl.BlockSpec((1, d), lambda i: (0, 0))],
        out_specs=pl.BlockSpec((tm, d), lambda i: (i, 0)),
        compiler_params=_cparams(("parallel",)),
        name="combine_norm",
    )(h, y_pairs, route, gain.reshape(1, d))


def _pad_lanes(v, offset=0, width=LANES):
    return jnp.zeros((1, width), F32).at[0, offset:offset + v.shape[0]].set(v.astype(F32))


def forward(cfg, x, ev_norm_mix, ev_w_in, ev_ssd_conv_w, ev_ssd_conv_b, ev_ssd_a_log, ev_ssd_dt_bias,
            ev_ssd_d, ev_ssd_norm, ev_mlstm_conv_w, ev_mlstm_conv_b, ev_mlstm_b_i, ev_mlstm_b_f,
            ev_mlstm_norm, ev_w_out, ev_norm_ffn, ev_ffn_gate, ev_ffn_up, ev_ffn_down,
            od_norm_mix, od_w_in, od_hgrn_norm, od_w_out, od_norm_ffn, od_router,
            od_exp_gate, od_exp_up, od_exp_down, hgrn_lb_param, final_norm):
    d, t = cfg.d_model, cfg.tokens
    tm = min(cfg.tm, t)
    h = x.reshape(t, d)

    nh, mh = cfg.ssd_heads, cfg.mlstm_heads
    bcw = 2 * cfg.ssd_groups * cfg.ssd_state
    qw = mh * cfg.mlstm_qk
    sizes = (d, d + bcw, nh, qw, qw, mh * cfg.mlstm_v, mh * cfg.mlstm_v, mh, mh)
    cuts = [0]
    for s in sizes:
        cuts.append(cuts[-1] + s)
    w_in = ev_w_in[0]
    col = lambda a, b: w_in[:, a:b]
    z0, xbc0, dt0, q0, k0, v0, o0, i0, f0 = cuts[:9]
    w_main = jnp.concatenate([col(z0, z0 + d), col(xbc0, xbc0 + d), col(v0, o0), col(o0, i0),
                              col(xbc0 + d, dt0), col(q0, k0), col(k0, v0)], axis=1).astype(BF16)
    w_gate = jnp.zeros((d, 3 * LANES), F32)
    w_gate = w_gate.at[:, 0:nh].set(col(dt0, q0))
    w_gate = w_gate.at[:, LANES:LANES + mh].set(col(i0, f0))
    w_gate = w_gate.at[:, 2 * LANES:2 * LANES + mh].set(col(f0, cuts[9]))

    tn_proj = min(cfg.tn_proj, w_main.shape[1])
    while w_main.shape[1] % tn_proj:
        tn_proj //= 2
    proj = norm_matmul(h, ev_norm_mix[0], w_main, BF16, tm, tn_proj)
    gates = norm_matmul(h, ev_norm_mix[0], w_gate.astype(BF16), F32, tm, 3 * LANES)

    cw, cb = ev_ssd_conv_w[0], ev_ssd_conv_b[0]
    e_mat = (jnp.arange(LANES)[:, None] == (jnp.arange(d)[None, :] // cfg.ssd_head_dim)).astype(BF16)
    y_a = ssd_mixer(cfg, proj, gates, cw[:, :d], cb[None, :d], cw[:, d:], cb[None, d:],
                    _pad_lanes(ev_ssd_a_log[0]), _pad_lanes(ev_ssd_dt_bias[0]),
                    jnp.repeat(ev_ssd_d[0].astype(F32), cfg.ssd_head_dim)[None, :],
                    ev_ssd_norm[0][None, :], e_mat)
    mw, mb = ev_mlstm_conv_w[0], ev_mlstm_conv_b[0]
    y_b = mlstm_mixer(cfg, proj, gates, mw[:, :qw], mb[None, :qw], mw[:, qw:], mb[None, qw:],
                      _pad_lanes(ev_mlstm_b_i[0]), _pad_lanes(ev_mlstm_b_f[0]), ev_mlstm_norm[0][None, :])
    tn_out = min(cfg.tn_proj, d)
    h = matmul_res([y_a, y_b], ev_w_out[0].astype(BF16), h, tm, tn_out)

    n_tiles = t // tm
    h = swiglu_tiles(h, ev_norm_ffn[0], ev_ffn_gate.astype(BF16), ev_ffn_up.astype(BF16),
                     ev_ffn_down.astype(BF16), jnp.zeros((n_tiles,), jnp.int32),
                     jnp.full((1,), n_tiles, jnp.int32), tm, min(cfg.tn_ffn, cfg.d_ff_dense), dense=True)

    w2 = od_w_in[0]
    w_qig = jnp.concatenate([w2[:, 0:d], w2[:, 2 * d:3 * d], w2[:, 3 * d:4 * d]], axis=1).astype(BF16)
    proj2 = norm_matmul(h, od_norm_mix[0], w_qig, BF16, tm, tn_out)
    f_pre = norm_matmul(h, od_norm_mix[0], w2[:, d:2 * d].astype(BF16), F32, tm, tn_out)
    y_c = hgrn_mixer(cfg, proj2, f_pre, hgrn_lb_param.astype(F32), od_hgrn_norm[0][None, :])
    h = matmul_res([y_c], od_w_out[0].astype(BF16), h, tm, tn_out)

    ne = cfg.n_experts
    w_router_p = jnp.zeros((d, LANES), F32).at[:, :ne].set(od_router[0])
    xn, route, counts = router(h, od_norm_ffn[0], w_router_p, ne, tm)
    counts = counts[0, :ne].astype(jnp.int32)
    padded = ((counts + tm - 1) // tm) * tm
    ends = jnp.cumsum(padded)
    starts = ends - padded
    n_tiles_moe = (cfg.top_k * t) // tm + ne
    n_rows = n_tiles_moe * tm
    idx = route[:, 0:2].astype(jnp.int32)
    rank = route[:, 2:4].astype(jnp.int32)
    dest = (starts[idx] + rank).reshape(-1)
    tile_expert = jnp.minimum(
        jnp.sum((jnp.arange(n_tiles_moe)[:, None] * tm) >= ends[None, :], axis=1), ne - 1).astype(jnp.int32)
    n_active = (ends[ne - 1:ne] // tm).astype(jnp.int32)
    rps = min(cfg.rows_per_dma_step, t)
    xs = scatter_rows(xn, dest, n_rows, rps, cfg.top_k)
    ys = swiglu_tiles(xs, od_norm_ffn[0], od_exp_gate[0], od_exp_up[0], od_exp_down[0],
                      tile_expert, n_active, tm, min(cfg.tn_moe, cfg.d_ff_expert), dense=False)
    y_pairs = gather_rows(ys, dest, t, rps, cfg.top_k)
    out = combine_norm(h, y_pairs, route, final_norm, min(tm, 256))
    return out.reshape(cfg.batch, cfg.seq, d)


def kernel(x, ev_norm_mix, ev_w_in, ev_ssd_conv_w, ev_ssd_conv_b, ev_ssd_a_log, ev_ssd_dt_bias, ev_ssd_d, ev_ssd_norm, ev_mlstm_conv_w, ev_mlstm_conv_b, ev_mlstm_b_i, ev_mlstm_b_f, ev_mlstm_norm, ev_w_out, ev_norm_ffn, ev_ffn_gate, ev_ffn_up, ev_ffn_down, od_norm_mix, od_w_in, od_hgrn_norm, od_w_out, od_norm_ffn, od_router, od_exp_gate, od_exp_up, od_exp_down, hgrn_lb_param, final_norm):
    return forward(Cfg(), x, ev_norm_mix, ev_w_in, ev_ssd_conv_w, ev_ssd_conv_b, ev_ssd_a_log, ev_ssd_dt_bias,
                   ev_ssd_d, ev_ssd_norm, ev_mlstm_conv_w, ev_mlstm_conv_b, ev_mlstm_b_i, ev_mlstm_b_f,
                   ev_mlstm_norm, ev_w_out, ev_norm_ffn, ev_ffn_gate, ev_ffn_up, ev_ffn_down,
                   od_norm_mix, od_w_in, od_hgrn_norm, od_w_out, od_norm_ffn, od_router,
                   od_exp_gate, od_exp_up, od_exp_down, hgrn_lb_param, final_norm)
```

```python
import functools
from typing import NamedTuple

import jax
import jax.numpy as jnp
from jax import lax
from jax.experimental import pallas as pl
from jax.experimental.pallas import tpu as pltpu

F32 = jnp.float32
BF16 = jnp.bfloat16

NORM_EPS = 1e-6
NEG_INF = float("-inf")

LANES = 128
SUBLANES = 8
VMEM_LIMIT_BYTES = 60 * 1024 * 1024


class Cfg(NamedTuple):
    d_model: int = 2048
    batch: int = 8
    seq: int = 2048
    conv_width: int = 4
    ssd_head_dim: int = 64
    ssd_group_heads: int = 8
    ssd_state: int = 128
    mlstm_qk: int = 128
    mlstm_v: int = 256
    hgrn_dk: int = 128
    d_ff_dense: int = 5632
    n_experts: int = 8
    top_k: int = 2
    d_ff_expert: int = 7168
    chunk: int = 128
    hgrn_chunk: int = 64
    hgrn_sub: int = 16
    tm: int = 1024
    tn_proj: int = 1024
    tn_ffn: int = 512
    tn_moe: int = 256
    rows_per_combine_step: int = 256

    @property
    def tokens(self):
        return self.batch * self.seq

    @property
    def ssd_heads(self):
        return self.d_model // self.ssd_head_dim

    @property
    def ssd_groups(self):
        return self.ssd_heads // self.ssd_group_heads

    @property
    def mlstm_heads(self):
        return self.d_model // self.mlstm_v

    @property
    def hgrn_heads(self):
        return self.d_model // self.hgrn_dk


def _cparams(sem):
    return pltpu.CompilerParams(dimension_semantics=sem, vmem_limit_bytes=VMEM_LIMIT_BYTES)


def _dot(a, b):
    return jnp.dot(a, b, preferred_element_type=F32)


def _dot_nt(a, b):
    return lax.dot_general(a, b, (((1,), (1,)), ((), ())), preferred_element_type=F32)


def _dot_tn(a, b):
    return lax.dot_general(a, b, (((0,), (0,)), ((), ())), preferred_element_type=F32)


def _sigmoid(x):
    return 1.0 / (1.0 + jnp.exp(-x))


def _silu(x):
    return x * _sigmoid(x)


def _softplus(x):
    return jnp.maximum(x, 0.0) + jnp.log(1.0 + jnp.exp(-jnp.abs(x)))


def _cumsum_rows(x):
    n = x.shape[0]
    row = lax.broadcasted_iota(jnp.int32, x.shape, 0)
    s = 1
    while s < n:
        x = x + jnp.where(row >= s, pltpu.roll(x, s, axis=0), 0.0)
        s *= 2
    return x


def _expand_heads(v, e):
    hi = v.astype(BF16)
    lo = (v - hi.astype(F32)).astype(BF16)
    return _dot(hi, e) + _dot(lo, e)


def _group_rmsnorm(y, gain_row, group):
    parts = []
    for j in range(y.shape[1] // group):
        yj = y[:, j * group:(j + 1) * group]
        ms = jnp.mean(yj * yj, axis=-1, keepdims=True)
        parts.append(yj * lax.rsqrt(ms + NORM_EPS))
    return jnp.concatenate(parts, axis=1) * gain_row


def _rmsnorm_rows_to(x_ref, gain_ref, out_ref, rows_per_step=128):
    n = x_ref.shape[0]
    step = min(rows_per_step, n)

    def body(i, carry):
        r = pl.multiple_of(i * step, step)
        x = x_ref[pl.ds(r, step), :].astype(F32)
        ms = jnp.mean(x * x, axis=-1, keepdims=True)
        out_ref[pl.ds(r, step), :] = (x * lax.rsqrt(ms + NORM_EPS) * gain_ref[...]).astype(out_ref.dtype)
        return carry

    lax.fori_loop(0, n // step, body, 0)


def _rows_to_slabs(x_ref, slab_ref, rows_per_step=128):
    n, width = x_ref.shape
    s = width // LANES
    step = min(rows_per_step, n)

    def body(i, carry):
        r = pl.multiple_of(i * step, step)
        for c in range(s):
            slab_ref[pl.ds(r * s + c, step, stride=s), :] = (
                x_ref[pl.ds(r, step), c * LANES:(c + 1) * LANES].astype(slab_ref.dtype))
        return carry

    lax.fori_loop(0, n // step, body, 0)


def _slabs_to_rows(slab_ref, x_ref, rows_per_step=128):
    n, width = x_ref.shape
    s = width // LANES
    step = min(rows_per_step, n)

    def body(i, carry):
        r = pl.multiple_of(i * step, step)
        for c in range(s):
            x_ref[pl.ds(r, step), c * LANES:(c + 1) * LANES] = (
                slab_ref[pl.ds(r * s + c, step, stride=s), :].astype(x_ref.dtype))
        return carry

    lax.fori_loop(0, n // step, body, 0)


def _causal_conv(x, tail_ref, w_ref, b_ref):
    n = x.shape[0]
    width = w_ref.shape[0]
    tail = tail_ref[...]
    row8 = lax.broadcasted_iota(jnp.int32, tail.shape, 0)
    acc = x * w_ref[width - 1:width, :] + b_ref[...]
    for k in range(1, width):
        rolled = pltpu.roll(x, k, axis=0)
        head = jnp.where(row8 < k, pltpu.roll(tail, k, axis=0), rolled[:SUBLANES])
        shifted = jnp.concatenate([head, rolled[SUBLANES:]], axis=0)
        acc = acc + shifted * w_ref[width - 1 - k:width - k, :]
    tail_ref[...] = x[n - SUBLANES:]
    return acc


def _norm_matmul_kernel(x_ref, g_ref, w_ref, o_ref, xn_ref):
    @pl.when(pl.program_id(1) == 0)
    def _():
        _rmsnorm_rows_to(x_ref, g_ref, xn_ref)

    o_ref[...] = _dot(xn_ref[...], w_ref[...].astype(BF16)).astype(o_ref.dtype)


def norm_matmul(x, gain, w, out_dtype, tm, tn):
    t, d = x.shape
    n = w.shape[1]
    return pl.pallas_call(
        _norm_matmul_kernel,
        out_shape=jax.ShapeDtypeStruct((t, n), out_dtype),
        grid=(t // tm, n // tn),
        in_specs=[pl.BlockSpec((tm, d), lambda i, j: (i, 0)),
                  pl.BlockSpec((1, d), lambda i, j: (0, 0)),
                  pl.BlockSpec((d, tn), lambda i, j: (0, j))],
        out_specs=pl.BlockSpec((tm, tn), lambda i, j: (i, j)),
        scratch_shapes=[pltpu.VMEM((tm, d), BF16)],
        compiler_params=_cparams(("parallel", "arbitrary")),
        name="norm_matmul",
    )(x, gain.reshape(1, d), w)


def _matmul_res_kernel(*refs, n_in):
    a_refs = refs[:n_in]
    w_refs = refs[n_in:2 * n_in]
    res_ref = refs[2 * n_in]
    o_ref = refs[2 * n_in + 1]
    acc = res_ref[...]
    for a_ref, w_ref in zip(a_refs, w_refs):
        acc = acc + _dot(a_ref[...], w_ref[...].astype(BF16))
    o_ref[...] = acc


def matmul_res(a_list, w, res, tm, tn):
    n_in = len(a_list)
    t, k = a_list[0].shape
    n = w.shape[1]
    in_specs = [pl.BlockSpec((tm, k), lambda i, j: (i, 0)) for _ in range(n_in)]
    in_specs += [pl.BlockSpec((k, tn), functools.partial(lambda i, j, s: (s, j), s=s)) for s in range(n_in)]
    in_specs += [pl.BlockSpec((tm, tn), lambda i, j: (i, j))]
    return pl.pallas_call(
        functools.partial(_matmul_res_kernel, n_in=n_in),
        out_shape=jax.ShapeDtypeStruct((t, n), F32),
        grid=(t // tm, n // tn),
        in_specs=in_specs,
        out_specs=pl.BlockSpec((tm, tn), lambda i, j: (i, j)),
        compiler_params=_cparams(("parallel", "arbitrary")),
        name="matmul_res",
    )(*a_list, *([w] * n_in), res)


def _ssd_kernel(z_ref, xs_ref, bc_ref, dt_ref, cwx_ref, cbx_ref, cwb_ref, cbb_ref, alog_ref,
                dtb_ref, dskip_ref, ng_ref, e_ref, o_ref, state_ref, tailx_ref, tailb_ref,
                *, n_heads, group_heads, head_dim, n_state):
    @pl.when(pl.program_id(1) == 0)
    def _():
        state_ref[...] = jnp.zeros_like(state_ref)
        tailx_ref[...] = jnp.zeros_like(tailx_ref)
        tailb_ref[...] = jnp.zeros_like(tailb_ref)

    rows = xs_ref.shape[0]
    n_groups = n_heads // group_heads
    gw = group_heads * head_dim
    xs = _silu(_causal_conv(xs_ref[...].astype(F32), tailx_ref, cwx_ref, cbx_ref))
    bc = _silu(_causal_conv(bc_ref[...].astype(F32), tailb_ref, cwb_ref, cbb_ref))

    lane = lax.broadcasted_iota(jnp.int32, (rows, LANES), 1)
    dt = jnp.where(lane < n_heads, _softplus(dt_ref[...] + dtb_ref[...]), 0.0)
    la = dt * (-jnp.exp(alog_ref[...]))
    acum = _cumsum_rows(la)
    acum_t = acum.T
    a_last = acum[rows - 1:rows, :]
    e = e_ref[...]
    dt_x = _expand_heads(dt, e)
    eacum_x = _expand_heads(jnp.exp(acum), e)
    dend_x = _expand_heads(jnp.exp(a_last - acum), e)
    cdec_x = eacum_x[rows - 1:rows, :]

    xdt = xs * dt_x
    xdt_b = xdt.astype(BF16)
    xdend_b = (xdt * dend_x).astype(BF16)
    state = state_ref[...]
    state_b = state.astype(BF16)

    r_i = lax.broadcasted_iota(jnp.int32, (rows, rows), 0)
    c_i = lax.broadcasted_iota(jnp.int32, (rows, rows), 1)
    causal = r_i >= c_i
    lane_lo = lane < (LANES // 2)

    y_diag, y_off, st_new = [], [], []
    heads_per_tile = LANES // head_dim
    for g in range(n_groups):
        b_g = bc[:, g * n_state:(g + 1) * n_state].astype(BF16)
        c_g = bc[:, (n_groups + g) * n_state:(n_groups + g + 1) * n_state].astype(BF16)
        cb = _dot_nt(c_g, b_g)
        y_off.append(_dot(c_g, state_b[:, g * gw:(g + 1) * gw]))
        st_new.append(_dot_tn(b_g, xdend_b[:, g * gw:(g + 1) * gw]))
        for p in range(group_heads // heads_per_tile):
            h0 = g * group_heads + p * heads_per_tile
            ms = []
            for h in range(h0, h0 + heads_per_tile):
                seg = acum[:, h:h + 1] - acum_t[h:h + 1, :]
                dec = jnp.exp(jnp.where(causal, seg, NEG_INF))
                ms.append((cb * dec).astype(BF16))
            lhs = jnp.concatenate(ms, axis=1)
            xp = xdt_b[:, h0 * head_dim:h0 * head_dim + LANES]
            zero = jnp.zeros_like(xp)
            rhs = jnp.concatenate([jnp.where(lane_lo, xp, zero), jnp.where(lane_lo, zero, xp)], axis=0)
            y_diag.append(_dot(lhs, rhs))
    y = (jnp.concatenate(y_diag, axis=1) + jnp.concatenate(y_off, axis=1) * eacum_x
         + dskip_ref[...] * xs)
    state_ref[...] = state * cdec_x + jnp.concatenate(st_new, axis=1)
    y = y * _silu(z_ref[...].astype(F32))
    o_ref[...] = _group_rmsnorm(y, ng_ref[...], gw).astype(o_ref.dtype)


def ssd_mixer(cfg, proj, gates, cwx, cbx, cwb, cbb, a_log_p, dt_bias_p, d_skip_x, norm_g, e_mat):
    d, L = cfg.d_model, cfg.chunk
    nc = cfg.seq // L
    bcw = 2 * cfg.ssd_groups * cfg.ssd_state
    row = lambda b, c: b * nc + c
    full = lambda shape: pl.BlockSpec(shape, lambda b, c: (0, 0))
    kern = functools.partial(_ssd_kernel, n_heads=cfg.ssd_heads, group_heads=cfg.ssd_group_heads,
                             head_dim=cfg.ssd_head_dim, n_state=cfg.ssd_state)
    return pl.pallas_call(
        kern,
        out_shape=jax.ShapeDtypeStruct((cfg.tokens, d), BF16),
        grid=(cfg.batch, nc),
        in_specs=[pl.BlockSpec((L, d), lambda b, c: (row(b, c), 0)),
                  pl.BlockSpec((L, d), lambda b, c: (row(b, c), 1)),
                  pl.BlockSpec((L, bcw), lambda b, c: (row(b, c), 4 * d // bcw)),
                  pl.BlockSpec((L, LANES), lambda b, c: (row(b, c), 0)),
                  full(cwx.shape), full(cbx.shape), full(cwb.shape), full(cbb.shape),
                  full(a_log_p.shape), full(dt_bias_p.shape), full(d_skip_x.shape),
                  full(norm_g.shape), full(e_mat.shape)],
        out_specs=pl.BlockSpec((L, d), lambda b, c: (row(b, c), 0)),
        scratch_shapes=[pltpu.VMEM((cfg.ssd_state, d), F32),
                        pltpu.VMEM((SUBLANES, d), F32),
                        pltpu.VMEM((SUBLANES, bcw), F32)],
        compiler_params=_cparams(("parallel", "arbitrary")),
        name="ssd_mixer",
    )(proj, proj, proj, gates, cwx, cbx, cwb, cbb, a_log_p, dt_bias_p, d_skip_x, norm_g, e_mat)


def _mlstm_kernel(q_ref, k_ref, v_ref, og_ref, ig_ref, fg_ref, cwq_ref, cbq_ref, cwk_ref, cbk_ref,
                  bi_ref, bf_ref, ng_ref, o_ref, c_ref, n_ref, m_ref, tailq_ref, tailk_ref,
                  *, n_heads, dk, dv):
    @pl.when(pl.program_id(1) == 0)
    def _():
        c_ref[...] = jnp.zeros_like(c_ref)
        n_ref[...] = jnp.zeros_like(n_ref)
        m_ref[...] = jnp.full(m_ref.shape, NEG_INF, F32)
        tailq_ref[...] = jnp.zeros_like(tailq_ref)
        tailk_ref[...] = jnp.zeros_like(tailk_ref)

    rows = q_ref.shape[0]
    q = _silu(_causal_conv(q_ref[...].astype(F32), tailq_ref, cwq_ref, cbq_ref))
    k = _silu(_causal_conv(k_ref[...].astype(F32), tailk_ref, cwk_ref, cbk_ref)) * (dk ** -0.5)
    i_log = ig_ref[...] + bi_ref[...]
    f_log = -_softplus(-(fg_ref[...] + bf_ref[...]))
    b = _cumsum_rows(f_log)
    b_t = b.T
    i_t = i_log.T
    g_end = b[rows - 1:rows, :] - b + i_log
    r_i = lax.broadcasted_iota(jnp.int32, (rows, rows), 0)
    c_i = lax.broadcasted_iota(jnp.int32, (rows, rows), 1)
    causal = r_i >= c_i
    rep = dv // LANES

    def wide(x):
        return jnp.concatenate([x] * rep, axis=1)

    outs = []
    for h in range(n_heads):
        qh = q[:, h * dk:(h + 1) * dk]
        kh = k[:, h * dk:(h + 1) * dk]
        vh = v_ref[:, h * dv:(h + 1) * dv]
        qb = qh.astype(BF16)
        b_col = jnp.broadcast_to(b[:, h:h + 1], (rows, LANES))
        log_d = jnp.where(causal, b_col - b_t[h:h + 1, :] + i_t[h:h + 1, :], NEG_INF)
        m_intra = jnp.max(log_d, axis=-1, keepdims=True)
        m_prev = m_ref[h:h + 1, :]
        m_inter = b_col + m_prev
        m_t = jnp.maximum(m_inter, m_intra)
        a_inter = jnp.exp(m_inter - m_t)
        pw = jnp.exp(log_d - m_t) * _dot_nt(qb, kh.astype(BF16))
        c_prev = c_ref[h]
        n_prev = n_ref[h:h + 1, :]
        num = _dot(pw.astype(BF16), vh) + wide(a_inter) * _dot(qb, c_prev.astype(BF16))
        den = (jnp.sum(pw, axis=-1, keepdims=True)
               + a_inter * jnp.sum(qh * n_prev, axis=-1, keepdims=True))
        outs.append(num / wide(jnp.maximum(jnp.abs(den), jnp.exp(-m_t))))
        ge = jnp.broadcast_to(g_end[:, h:h + 1], (rows, LANES))
        m_loc = jnp.max(ge, axis=0, keepdims=True)
        kw = kh * jnp.exp(ge - m_loc)
        f_tot = b_col[rows - 1:rows, :]
        m_new = jnp.maximum(f_tot + m_prev, m_loc)
        a_prev = jnp.exp(f_tot + m_prev - m_new)
        a_loc = jnp.exp(m_loc - m_new)
        c_ref[h] = c_prev * wide(a_prev) + _dot_tn(kw.astype(BF16), vh) * wide(a_loc)
        n_ref[h:h + 1, :] = n_prev * a_prev + jnp.sum(kw, axis=0, keepdims=True) * a_loc
        m_ref[h:h + 1, :] = m_new
    hcat = jnp.concatenate(outs, axis=1) * _sigmoid(og_ref[...].astype(F32))
    o_ref[...] = _group_rmsnorm(hcat, ng_ref[...], dv).astype(o_ref.dtype)


def mlstm_mixer(cfg, proj, gates, cwq, cbq, cwk, cbk, b_i_p, b_f_p, norm_g):
    d, L = cfg.d_model, cfg.chunk
    nc = cfg.seq // L
    qw = cfg.mlstm_heads * cfg.mlstm_qk
    row = lambda b, c: b * nc + c
    full = lambda shape: pl.BlockSpec(shape, lambda b, c: (0, 0))
    kern = functools.partial(_mlstm_kernel, n_heads=cfg.mlstm_heads, dk=cfg.mlstm_qk, dv=cfg.mlstm_v)
    q_blk = (4 * d + 2 * cfg.ssd_groups * cfg.ssd_state) // qw
    return pl.pallas_call(
        kern,
        out_shape=jax.ShapeDtypeStruct((cfg.tokens, d), BF16),
        grid=(cfg.batch, nc),
        in_specs=[pl.BlockSpec((L, qw), lambda b, c: (row(b, c), q_blk)),
                  pl.BlockSpec((L, qw), lambda b, c: (row(b, c), q_blk + 1)),
                  pl.BlockSpec((L, d), lambda b, c: (row(b, c), 2)),
                  pl.BlockSpec((L, d), lambda b, c: (row(b, c), 3)),
                  pl.BlockSpec((L, LANES), lambda b, c: (row(b, c), 1)),
                  pl.BlockSpec((L, LANES), lambda b, c: (row(b, c), 2)),
                  full(cwq.shape), full(cbq.shape), full(cwk.shape), full(cbk.shape),
                  full(b_i_p.shape), full(b_f_p.shape), full(norm_g.shape)],
        out_specs=pl.BlockSpec((L, d), lambda b, c: (row(b, c), 0)),
        scratch_shapes=[pltpu.VMEM((cfg.mlstm_heads, cfg.mlstm_qk, cfg.mlstm_v), F32),
                        pltpu.VMEM((cfg.mlstm_heads, cfg.mlstm_qk), F32),
                        pltpu.VMEM((cfg.mlstm_heads, LANES), F32),
                        pltpu.VMEM((SUBLANES, qw), F32),
                        pltpu.VMEM((SUBLANES, qw), F32)],
        compiler_params=_cparams(("parallel", "arbitrary")),
        name="mlstm_mixer",
    )(proj, proj, proj, proj, gates, gates, cwq, cbq, cwk, cbk, b_i_p, b_f_p, norm_g)


def _hgrn_kernel(q_ref, f_ref, v_ref, g_ref, lbp_ref, ng_ref, o_ref,
                 st_ref, q_s, k_s, gc_s, v_s, o_s, *, n_heads, dk, sub):
    @pl.when(pl.program_id(1) == 0)
    def _():
        st_ref[...] = jnp.zeros_like(st_ref)

    rows = q_ref.shape[0]
    a0 = lbp_ref[0:1, :]
    a1 = lbp_ref[1:2, :]
    amax = jnp.maximum(a0, a1)
    e0 = jnp.exp(a0 - amax)
    e1 = jnp.exp(a1 - amax)
    p0 = e0 / (e0 + e1)
    p1 = e1 / (e0 + e1)
    lb = (p0 + p1) - p0
    f = lb + (1.0 - lb) * _sigmoid(f_ref[...])
    k_s[...] = 1.0 - f
    gc_s[...] = _cumsum_rows(jnp.log(f))
    q_s[...] = q_ref[...].astype(F32)
    v_s[...] = v_ref[...].astype(F32)
    row_sub = lax.broadcasted_iota(jnp.int32, (sub, dk), 0)

    def head(h, carry):
        off = pl.multiple_of(h * dk, dk)
        q = q_s[:, pl.ds(off, dk)]
        k = k_s[:, pl.ds(off, dk)]
        gc = gc_s[:, pl.ds(off, dk)]
        v = v_s[:, pl.ds(off, dk)]
        v_b = v.astype(BF16)
        st = st_ref[h]
        g_end = gc[rows - 1:rows, :]
        o = _dot_nt((q * jnp.exp(gc)).astype(BF16), st.astype(BF16))
        k_end = (k * jnp.exp(g_end - gc)).astype(BF16)
        st_ref[h] = st * jnp.exp(g_end) + _dot_tn(v_b, k_end)
        blocks = []
        for i in range(rows // sub):
            r0 = i * sub
            qi = q[r0:r0 + sub]
            gi = gc[r0:r0 + sub]
            ki = k[r0:r0 + sub]
            vi = v[r0:r0 + sub]
            oi = o[r0:r0 + sub]
            if i > 0:
                gref = gc[r0 - 1:r0, :]
                qt = (qi * jnp.exp(gi - gref)).astype(BF16)
                kt = (k[:r0] * jnp.exp(gref - gc[:r0])).astype(BF16)
                att = _dot_nt(qt, kt)
                oi = oi + _dot(att.astype(BF16), v_b[:r0])
            for s in range(sub):
                dec = jnp.exp(jnp.where(row_sub >= s, gi - gi[s:s + 1, :], NEG_INF))
                att_s = jnp.sum(qi * dec * ki[s:s + 1, :], axis=-1, keepdims=True)
                oi = oi + att_s * vi[s:s + 1, :]
            blocks.append(oi)
        o_s[:, pl.ds(off, dk)] = jnp.concatenate(blocks, axis=0)
        return carry

    lax.fori_loop(0, n_heads, head, 0, unroll=4)
    y = _group_rmsnorm(o_s[...], ng_ref[...], dk) * _silu(g_ref[...].astype(F32))
    o_ref[...] = y.astype(o_ref.dtype)


def hgrn_mixer(cfg, proj, f_pre, lb_param, norm_g):
    d, L = cfg.d_model, cfg.hgrn_chunk
    nc = cfg.seq // L
    row = lambda b, c: b * nc + c
    full = lambda shape: pl.BlockSpec(shape, lambda b, c: (0, 0))
    kern = functools.partial(_hgrn_kernel, n_heads=cfg.hgrn_heads, dk=cfg.hgrn_dk, sub=cfg.hgrn_sub)
    return pl.pallas_call(
        kern,
        out_shape=jax.ShapeDtypeStruct((cfg.tokens, d), BF16),
        grid=(cfg.batch, nc),
        in_specs=[pl.BlockSpec((L, d), lambda b, c: (row(b, c), 0)),
                  pl.BlockSpec((L, d), lambda b, c: (row(b, c), 0)),
                  pl.BlockSpec((L, d), lambda b, c: (row(b, c), 1)),
                  pl.BlockSpec((L, d), lambda b, c: (row(b, c), 2)),
                  full(lb_param.shape), full(norm_g.shape)],
        out_specs=pl.BlockSpec((L, d), lambda b, c: (row(b, c), 0)),
        scratch_shapes=[pltpu.VMEM((cfg.hgrn_heads, cfg.hgrn_dk, cfg.hgrn_dk), F32)]
                       + [pltpu.VMEM((L, d), F32)] * 5,
        compiler_params=_cparams(("parallel", "arbitrary")),
        name="hgrn_mixer",
    )(proj, f_pre, proj, proj, lb_param, norm_g)


def _swiglu_step(xn, wg, wu, wd):
    gate = _dot(xn, wg.astype(BF16))
    up = _dot(xn, wu.astype(BF16))
    return _dot((_silu(gate) * up).astype(BF16), wd.astype(BF16))


def _ffn_kernel(x_ref, gain_ref, wg_ref, wu_ref, wd_ref, o_ref, xn_ref):
    @pl.when(pl.program_id(1) == 0)
    def _():
        _rmsnorm_rows_to(x_ref, gain_ref, xn_ref)
        o_ref[...] = x_ref[...]

    o_ref[...] += _swiglu_step(xn_ref[...], wg_ref[...], wu_ref[...], wd_ref[...])


def swiglu_dense(x, gain, w_gate, w_up, w_down, tm, tn):
    t, d = x.shape
    f = w_gate.shape[1]
    return pl.pallas_call(
        _ffn_kernel,
        out_shape=jax.ShapeDtypeStruct((t, d), F32),
        grid=(t // tm, f // tn),
        in_specs=[pl.BlockSpec((tm, d), lambda i, j: (i, 0)),
                  pl.BlockSpec((1, d), lambda i, j: (0, 0)),
                  pl.BlockSpec((d, tn), lambda i, j: (0, j)),
                  pl.BlockSpec((d, tn), lambda i, j: (0, j)),
                  pl.BlockSpec((tn, d), lambda i, j: (j, 0))],
        out_specs=pl.BlockSpec((tm, d), lambda i, j: (i, 0)),
        scratch_shapes=[pltpu.VMEM((tm, d), BF16)],
        compiler_params=_cparams(("parallel", "arbitrary")),
        name="swiglu_dense",
    )(x, gain.reshape(1, d), w_gate, w_up, w_down)


def _moe_kernel(te_ref, na_ref, src_ref, x3_hbm, wg_ref, wu_ref, wd_ref, o3_ref,
                stage_ref, xn_ref, acc_ref, sem):
    del te_ref
    i = pl.program_id(0)
    j = pl.program_id(1)
    n_active = na_ref[0]
    active = i < n_active
    tm, d = xn_ref.shape
    s = d // LANES

    def row_copy(tok, r):
        return pltpu.make_async_copy(x3_hbm.at[pl.ds(pl.multiple_of(tok * s, s), s)],
                                     stage_ref.at[pl.ds(pl.multiple_of(r * s, s), s)], sem)

    def start_gather(tile):
        def body(r, carry):
            row_copy(src_ref[tile * tm + r], r).start()
            return carry
        lax.fori_loop(0, tm, body, 0)

    def wait_gather():
        def body(r, carry):
            row_copy(0, r).wait()
            return carry
        lax.fori_loop(0, tm, body, 0)

    @pl.when(jnp.logical_and(j == 0, i == 0))
    def _():
        start_gather(0)

    @pl.when(jnp.logical_and(j == 0, active))
    def _():
        wait_gather()
        _slabs_to_rows(stage_ref, xn_ref)
        acc_ref[...] = jnp.zeros_like(acc_ref)

    @pl.when(jnp.logical_and(j == 0, i + 1 < n_active))
    def _():
        start_gather(i + 1)

    @pl.when(active)
    def _():
        acc_ref[...] += _swiglu_step(xn_ref[...], wg_ref[0], wu_ref[0], wd_ref[0])

    @pl.when(j == pl.num_programs(1) - 1)
    def _():
        _rows_to_slabs(acc_ref, o3_ref)


def swiglu_moe(x3, src_tok, w_gate, w_up, w_down, tile_expert, n_active, n_tiles, tm, tn):
    d = w_gate.shape[1]
    s = d // LANES
    f = w_gate.shape[2]
    grid_spec = pltpu.PrefetchScalarGridSpec(
        num_scalar_prefetch=3,
        grid=(n_tiles, f // tn),
        in_specs=[pl.BlockSpec(memory_space=pl.ANY),
                  pl.BlockSpec((1, d, tn), lambda i, j, te, na, src: (te[i], 0, j)),
                  pl.BlockSpec((1, d, tn), lambda i, j, te, na, src: (te[i], 0, j)),
                  pl.BlockSpec((1, tn, d), lambda i, j, te, na, src: (te[i], j, 0))],
        out_specs=pl.BlockSpec((tm * s, LANES), lambda i, j, te, na, src: (i, 0)),
        scratch_shapes=[pltpu.VMEM((tm * s, LANES), F32),
                        pltpu.VMEM((tm, d), BF16),
                        pltpu.VMEM((tm, d), F32),
                        pltpu.SemaphoreType.DMA(())],
    )
    return pl.pallas_call(
        _moe_kernel,
        out_shape=jax.ShapeDtypeStruct((n_tiles * tm * s, LANES), F32),
        grid_spec=grid_spec,
        compiler_params=_cparams(("arbitrary", "arbitrary")),
        name="swiglu_moe",
    )(tile_expert, n_active, src_tok, x3, w_gate, w_up, w_down)


def _router_kernel(x_ref, gain_ref, wr_ref, x3_ref, route_ref, count_ref, xn_ref, run_ref, tri_ref,
                   *, n_experts):
    i = pl.program_id(0)
    rows = x_ref.shape[0]

    @pl.when(i == 0)
    def _():
        run_ref[...] = jnp.zeros_like(run_ref)
        r_i = lax.broadcasted_iota(jnp.int32, (rows, rows), 0)
        c_i = lax.broadcasted_iota(jnp.int32, (rows, rows), 1)
        tri_ref[...] = jnp.where(r_i > c_i, 1.0, 0.0).astype(BF16)

    _rmsnorm_rows_to(x_ref, gain_ref, xn_ref)
    _rows_to_slabs(xn_ref, x3_ref)
    xn = xn_ref[...]
    x_hi = xn.astype(BF16)
    x_lo = (xn - x_hi.astype(F32)).astype(BF16)
    w = wr_ref[...]
    w_hi = w.astype(BF16)
    w_lo = (w - w_hi.astype(F32)).astype(BF16)
    logits = _dot(x_hi, w_hi) + (_dot(x_lo, w_hi) + _dot(x_hi, w_lo))

    lane = lax.broadcasted_iota(jnp.int32, logits.shape, 1)
    lane_f = lane.astype(F32)
    big = float(LANES)
    l1 = jnp.where(lane < n_experts, logits, NEG_INF)
    m1 = jnp.max(l1, axis=-1, keepdims=True)
    idx1 = jnp.min(jnp.where(l1 == m1, lane_f, big), axis=-1, keepdims=True)
    hot1 = lane_f == idx1
    l2 = jnp.where(hot1, NEG_INF, l1)
    m2 = jnp.max(l2, axis=-1, keepdims=True)
    idx2 = jnp.min(jnp.where(l2 == m2, lane_f, big), axis=-1, keepdims=True)
    hot2 = lane_f == idx2
    e2 = jnp.exp(m2 - m1)
    g1 = 1.0 / (1.0 + e2)
    g2 = e2 / (1.0 + e2)

    cnt = jnp.where(hot1, 1.0, 0.0) + jnp.where(hot2, 1.0, 0.0)
    before = _dot(tri_ref[...], cnt.astype(BF16)) + run_ref[...]
    rank1 = jnp.sum(jnp.where(hot1, before, 0.0), axis=-1, keepdims=True)
    rank2 = jnp.sum(jnp.where(hot2, before, 0.0), axis=-1, keepdims=True)
    run_ref[...] = run_ref[...] + jnp.sum(cnt, axis=0, keepdims=True)
    count_ref[...] = jnp.broadcast_to(run_ref[...], count_ref.shape)

    out = jnp.zeros(logits.shape, F32)
    for pos, val in enumerate((idx1, idx2, rank1, rank2, g1, g2)):
        out = jnp.where(lane == pos, val, out)
    route_ref[...] = out


def router(x, gain, w_router_p, n_experts, tm):
    t, d = x.shape
    return pl.pallas_call(
        functools.partial(_router_kernel, n_experts=n_experts),
        out_shape=(jax.ShapeDtypeStruct((t * (d // LANES), LANES), F32),
                   jax.ShapeDtypeStruct((t, LANES), F32),
                   jax.ShapeDtypeStruct((SUBLANES, LANES), F32)),
        grid=(t // tm,),
        in_specs=[pl.BlockSpec((tm, d), lambda i: (i, 0)),
                  pl.BlockSpec((1, d), lambda i: (0, 0)),
                  pl.BlockSpec((d, LANES), lambda i: (0, 0))],
        out_specs=(pl.BlockSpec((tm * (d // LANES), LANES), lambda i: (i, 0)),
                   pl.BlockSpec((tm, LANES), lambda i: (i, 0)),
                   pl.BlockSpec((SUBLANES, LANES), lambda i: (0, 0))),
        scratch_shapes=[pltpu.VMEM((tm, d), F32), pltpu.VMEM((1, LANES), F32), pltpu.VMEM((tm, tm), BF16)],
        compiler_params=_cparams(("arbitrary",)),
        name="router",
    )(x, gain.reshape(1, d), w_router_p)


def _combine_kernel(dest_ref, h_ref, y3_hbm, route_ref, gain_ref, o_ref, stage_ref, y_ref, sems, *, fanout):
    i = pl.program_id(0)
    n_steps = pl.num_programs(0)
    tc, d = h_ref.shape
    ns = d // LANES
    slot = lax.rem(i, 2)

    def row_copy(row, r, sl):
        return pltpu.make_async_copy(y3_hbm.at[pl.ds(pl.multiple_of(row * ns, ns), ns)],
                                     stage_ref.at[sl, pl.ds(pl.multiple_of(r * ns, ns), ns)], sems.at[sl])

    def start_gather(step, sl):
        def body(r, carry):
            for s in range(fanout):
                row_copy(dest_ref[(step * tc + r) * fanout + s], s * tc + r, sl).start()
            return carry
        lax.fori_loop(0, tc, body, 0)

    def wait_gather(sl):
        def body(r, carry):
            row_copy(0, r, sl).wait()
            return carry
        lax.fori_loop(0, fanout * tc, body, 0)

    @pl.when(i == 0)
    def _():
        start_gather(0, 0)

    @pl.when(i + 1 < n_steps)
    def _():
        start_gather(i + 1, 1 - slot)

    wait_gather(slot)
    route = route_ref[...]
    acc = h_ref[...]
    for s in range(fanout):
        _slabs_to_rows(stage_ref.at[slot, pl.ds(s * tc * ns, tc * ns)], y_ref)
        acc = acc + route[:, 4 + s:5 + s] * y_ref[...]
    ms = jnp.mean(acc * acc, axis=-1, keepdims=True)
    o_ref[...] = acc * lax.rsqrt(ms + NORM_EPS) * gain_ref[...]


def combine_norm(h, y3, dest, route, gain, tc, fanout):
    t, d = h.shape
    s = d // LANES
    grid_spec = pltpu.PrefetchScalarGridSpec(
        num_scalar_prefetch=1,
        grid=(t // tc,),
        in_specs=[pl.BlockSpec((tc, d), lambda i, dest: (i, 0)),
                  pl.BlockSpec(memory_space=pl.ANY),
                  pl.BlockSpec((tc, LANES), lambda i, dest: (i, 0)),
                  pl.BlockSpec((1, d), lambda i, dest: (0, 0))],
        out_specs=pl.BlockSpec((tc, d), lambda i, dest: (i, 0)),
        scratch_shapes=[pltpu.VMEM((2, fanout * tc * s, LANES), F32),
                        pltpu.VMEM((tc, d), F32),
                        pltpu.SemaphoreType.DMA((2,))],
    )
    return pl.pallas_call(
        functools.partial(_combine_kernel, fanout=fanout),
        out_shape=jax.ShapeDtypeStruct((t, d), F32),
        grid_spec=grid_spec,
        compiler_params=_cparams(("arbitrary",)),
        name="combine_norm",
    )(dest, h, y3, route, gain.reshape(1, d))


def _pad_lanes(v, offset=0, width=LANES):
    return jnp.zeros((1, width), F32).at[0, offset:offset + v.shape[0]].set(v.astype(F32))


def forward(cfg, x, ev_norm_mix, ev_w_in, ev_ssd_conv_w, ev_ssd_conv_b, ev_ssd_a_log, ev_ssd_dt_bias,
            ev_ssd_d, ev_ssd_norm, ev_mlstm_conv_w, ev_mlstm_conv_b, ev_mlstm_b_i, ev_mlstm_b_f,
            ev_mlstm_norm, ev_w_out, ev_norm_ffn, ev_ffn_gate, ev_ffn_up, ev_ffn_down,
            od_norm_mix, od_w_in, od_hgrn_norm, od_w_out, od_norm_ffn, od_router,
            od_exp_gate, od_exp_up, od_exp_down, hgrn_lb_param, final_norm):
    d, t = cfg.d_model, cfg.tokens
    tm = min(cfg.tm, t)
    h = x.reshape(t, d)

    nh, mh = cfg.ssd_heads, cfg.mlstm_heads
    bcw = 2 * cfg.ssd_groups * cfg.ssd_state
    qw = mh * cfg.mlstm_qk
    sizes = (d, d + bcw, nh, qw, qw, mh * cfg.mlstm_v, mh * cfg.mlstm_v, mh, mh)
    cuts = [0]
    for s in sizes:
        cuts.append(cuts[-1] + s)
    w_in = ev_w_in[0]
    col = lambda a, b: w_in[:, a:b]
    z0, xbc0, dt0, q0, k0, v0, o0, i0, f0 = cuts[:9]
    w_main = jnp.concatenate([col(z0, z0 + d), col(xbc0, xbc0 + d), col(v0, o0), col(o0, i0),
                              col(xbc0 + d, dt0), col(q0, k0), col(k0, v0)], axis=1).astype(BF16)
    w_gate = jnp.zeros((d, 3 * LANES), F32)
    w_gate = w_gate.at[:, 0:nh].set(col(dt0, q0))
    w_gate = w_gate.at[:, LANES:LANES + mh].set(col(i0, f0))
    w_gate = w_gate.at[:, 2 * LANES:2 * LANES + mh].set(col(f0, cuts[9]))

    tn_proj = min(cfg.tn_proj, w_main.shape[1])
    while w_main.shape[1] % tn_proj:
        tn_proj //= 2
    proj = norm_matmul(h, ev_norm_mix[0], w_main, BF16, tm, tn_proj)
    gates = norm_matmul(h, ev_norm_mix[0], w_gate.astype(BF16), F32, tm, 3 * LANES)

    cw, cb = ev_ssd_conv_w[0], ev_ssd_conv_b[0]
    e_mat = (jnp.arange(LANES)[:, None] == (jnp.arange(d)[None, :] // cfg.ssd_head_dim)).astype(BF16)
    y_a = ssd_mixer(cfg, proj, gates, cw[:, :d], cb[None, :d], cw[:, d:], cb[None, d:],
                    _pad_lanes(ev_ssd_a_log[0]), _pad_lanes(ev_ssd_dt_bias[0]),
                    jnp.repeat(ev_ssd_d[0].astype(F32), cfg.ssd_head_dim)[None, :],
                    ev_ssd_norm[0][None, :], e_mat)
    mw, mb = ev_mlstm_conv_w[0], ev_mlstm_conv_b[0]
    y_b = mlstm_mixer(cfg, proj, gates, mw[:, :qw], mb[None, :qw], mw[:, qw:], mb[None, qw:],
                      _pad_lanes(ev_mlstm_b_i[0]), _pad_lanes(ev_mlstm_b_f[0]), ev_mlstm_norm[0][None, :])
    tn_out = min(cfg.tn_proj, d)
    h = matmul_res([y_a, y_b], ev_w_out[0].astype(BF16), h, tm, tn_out)

    h = swiglu_dense(h, ev_norm_ffn[0], ev_ffn_gate[0].astype(BF16), ev_ffn_up[0].astype(BF16),
                     ev_ffn_down[0].astype(BF16), tm, min(cfg.tn_ffn, cfg.d_ff_dense))

    w2 = od_w_in[0]
    w_qig = jnp.concatenate([w2[:, 0:d], w2[:, 2 * d:3 * d], w2[:, 3 * d:4 * d]], axis=1).astype(BF16)
    proj2 = norm_matmul(h, od_norm_mix[0], w_qig, BF16, tm, tn_out)
    f_pre = norm_matmul(h, od_norm_mix[0], w2[:, d:2 * d].astype(BF16), F32, tm, tn_out)
    y_c = hgrn_mixer(cfg, proj2, f_pre, hgrn_lb_param.astype(F32), od_hgrn_norm[0][None, :])
    h = matmul_res([y_c], od_w_out[0].astype(BF16), h, tm, tn_out)

    ne = cfg.n_experts
    w_router_p = jnp.zeros((d, LANES), F32).at[:, :ne].set(od_router[0])
    x3, route, counts = router(h, od_norm_ffn[0], w_router_p, ne, tm)
    counts = counts[0, :ne].astype(jnp.int32)
    padded = ((counts + tm - 1) // tm) * tm
    ends = jnp.cumsum(padded)
    starts = ends - padded
    n_tiles_moe = (cfg.top_k * t) // tm + ne
    n_rows = n_tiles_moe * tm
    idx = route[:, 0:2].astype(jnp.int32)
    rank = route[:, 2:4].astype(jnp.int32)
    dest = (starts[idx] + rank).reshape(-1)
    tile_expert = jnp.minimum(
        jnp.sum((jnp.arange(n_tiles_moe)[:, None] * tm) >= ends[None, :], axis=1), ne - 1).astype(jnp.int32)
    n_active = (ends[ne - 1:ne] // tm).astype(jnp.int32)
    src_tok = jnp.zeros((n_rows,), jnp.int32).at[dest].set(jnp.arange(cfg.top_k * t, dtype=jnp.int32) // cfg.top_k)
    y3 = swiglu_moe(x3, src_tok, od_exp_gate[0], od_exp_up[0], od_exp_down[0],
                    tile_expert, n_active, n_tiles_moe, tm, min(cfg.tn_moe, cfg.d_ff_expert))
    out = combine_norm(h, y3, dest, route, final_norm, min(cfg.rows_per_combine_step, t), cfg.top_k)
    return out.reshape(cfg.batch, cfg.seq, d)


def kernel(x, ev_norm_mix, ev_w_in, ev_ssd_conv_w, ev_ssd_conv_b, ev_ssd_a_log, ev_ssd_dt_bias, ev_ssd_d, ev_ssd_norm, ev_mlstm_conv_w, ev_mlstm_conv_b, ev_mlstm_b_i, ev_mlstm_b_f, ev_mlstm_norm, ev_w_out, ev_norm_ffn, ev_ffn_gate, ev_ffn_up, ev_ffn_down, od_norm_mix, od_w_in, od_hgrn_norm, od_w_out, od_norm_ffn, od_router, od_exp_gate, od_exp_up, od_exp_down, hgrn_lb_param, final_norm):
    return forward(Cfg(), x, ev_norm_mix, ev_w_in, ev_ssd_conv_w, ev_ssd_conv_b, ev_ssd_a_log, ev_ssd_dt_bias,
                   ev_ssd_d, ev_ssd_norm, ev_mlstm_conv_w, ev_mlstm_conv_b, ev_mlstm_b_i, ev_mlstm_b_f,
                   ev_mlstm_norm, ev_w_out, ev_norm_ffn, ev_ffn_gate, ev_ffn_up, ev_ffn_down,
                   od_norm_mix, od_w_in, od_hgrn_norm, od_w_out, od_norm_ffn, od_router,
                   od_exp_gate, od_exp_up, od_exp_down, hgrn_lb_param, final_norm)
```

```python
import functools
from typing import NamedTuple

import jax
import jax.numpy as jnp
from jax import lax
from jax.experimental import pallas as pl
from jax.experimental.pallas import tpu as pltpu

F32 = jnp.float32
BF16 = jnp.bfloat16

NORM_EPS = 1e-6
NEG_INF = float("-inf")

LANES = 128
SUBLANES = 8
VMEM_LIMIT_BYTES = 60 * 1024 * 1024


class Cfg(NamedTuple):
    d_model: int = 2048
    batch: int = 8
    seq: int = 2048
    conv_width: int = 4
    ssd_head_dim: int = 64
    ssd_group_heads: int = 8
    ssd_state: int = 128
    mlstm_qk: int = 128
    mlstm_v: int = 256
    hgrn_dk: int = 128
    d_ff_dense: int = 5632
    n_experts: int = 8
    top_k: int = 2
    d_ff_expert: int = 7168
    chunk: int = 128
    hgrn_chunk: int = 64
    hgrn_sub: int = 16
    tm: int = 1024
    tn_proj: int = 1024
    tn_ffn: int = 512
    tn_moe: int = 256
    rows_per_combine_step: int = 256

    @property
    def tokens(self):
        return self.batch * self.seq

    @property
    def ssd_heads(self):
        return self.d_model // self.ssd_head_dim

    @property
    def ssd_groups(self):
        return self.ssd_heads // self.ssd_group_heads

    @property
    def mlstm_heads(self):
        return self.d_model // self.mlstm_v

    @property
    def hgrn_heads(self):
        return self.d_model // self.hgrn_dk


def _cparams(sem):
    return pltpu.CompilerParams(dimension_semantics=sem, vmem_limit_bytes=VMEM_LIMIT_BYTES)


def _dot(a, b):
    return jnp.dot(a, b, preferred_element_type=F32)


def _dot_nt(a, b):
    return lax.dot_general(a, b, (((1,), (1,)), ((), ())), preferred_element_type=F32)


def _dot_tn(a, b):
    return lax.dot_general(a, b, (((0,), (0,)), ((), ())), preferred_element_type=F32)


def _sigmoid(x):
    return 1.0 / (1.0 + jnp.exp(-x))


def _silu(x):
    return x * _sigmoid(x)


def _softplus(x):
    return jnp.maximum(x, 0.0) + jnp.log(1.0 + jnp.exp(-jnp.abs(x)))


def _cumsum_rows(x):
    n = x.shape[0]
    row = lax.broadcasted_iota(jnp.int32, x.shape, 0)
    s = 1
    while s < n:
        x = x + jnp.where(row >= s, pltpu.roll(x, s, axis=0), 0.0)
        s *= 2
    return x


def _expand_heads(v, e):
    hi = v.astype(BF16)
    lo = (v - hi.astype(F32)).astype(BF16)
    return _dot(hi, e) + _dot(lo, e)


def _group_rmsnorm(y, gain_row, group):
    parts = []
    for j in range(y.shape[1] // group):
        yj = y[:, j * group:(j + 1) * group]
        ms = jnp.mean(yj * yj, axis=-1, keepdims=True)
        parts.append(yj * lax.rsqrt(ms + NORM_EPS))
    return jnp.concatenate(parts, axis=1) * gain_row


def _rmsnorm_rows_to(x_ref, gain_ref, out_ref, rows_per_step=128):
    n = x_ref.shape[0]
    step = min(rows_per_step, n)

    def body(i, carry):
        r = pl.multiple_of(i * step, step)
        x = x_ref[pl.ds(r, step), :].astype(F32)
        ms = jnp.mean(x * x, axis=-1, keepdims=True)
        out_ref[pl.ds(r, step), :] = (x * lax.rsqrt(ms + NORM_EPS) * gain_ref[...]).astype(out_ref.dtype)
        return carry

    lax.fori_loop(0, n // step, body, 0)


def _rows_to_slabs(x_ref, slab_ref, rows_per_step=128):
    n, width = x_ref.shape
    s = width // LANES
    step = min(rows_per_step, n)

    def body(i, carry):
        r = pl.multiple_of(i * step, step)
        for c in range(s):
            slab_ref[pl.ds(r * s + c, step, stride=s), :] = (
                x_ref[pl.ds(r, step), c * LANES:(c + 1) * LANES].astype(slab_ref.dtype))
        return carry

    lax.fori_loop(0, n // step, body, 0)


def _slabs_to_rows(slab_ref, x_ref, rows_per_step=128):
    n, width = x_ref.shape
    s = width // LANES
    step = min(rows_per_step, n)

    def body(i, carry):
        r = pl.multiple_of(i * step, step)
        for c in range(s):
            x_ref[pl.ds(r, step), c * LANES:(c + 1) * LANES] = (
                slab_ref[pl.ds(r * s + c, step, stride=s), :].astype(x_ref.dtype))
        return carry

    lax.fori_loop(0, n // step, body, 0)


def _causal_conv(x, tail_ref, w_ref, b_ref):
    n = x.shape[0]
    width = w_ref.shape[0]
    tail = tail_ref[...]
    row8 = lax.broadcasted_iota(jnp.int32, tail.shape, 0)
    acc = x * w_ref[width - 1:width, :] + b_ref[...]
    for k in range(1, width):
        rolled = pltpu.roll(x, k, axis=0)
        head = jnp.where(row8 < k, pltpu.roll(tail, k, axis=0), rolled[:SUBLANES])
        shifted = jnp.concatenate([head, rolled[SUBLANES:]], axis=0)
        acc = acc + shifted * w_ref[width - 1 - k:width - k, :]
    tail_ref[...] = x[n - SUBLANES:]
    return acc


def _norm_matmul_kernel(x_ref, g_ref, w_ref, o_ref, xn_ref):
    @pl.when(pl.program_id(1) == 0)
    def _():
        _rmsnorm_rows_to(x_ref, g_ref, xn_ref)

    o_ref[...] = _dot(xn_ref[...], w_ref[...].astype(BF16)).astype(o_ref.dtype)


def norm_matmul(x, gain, w, out_dtype, tm, tn):
    t, d = x.shape
    n = w.shape[1]
    return pl.pallas_call(
        _norm_matmul_kernel,
        out_shape=jax.ShapeDtypeStruct((t, n), out_dtype),
        grid=(t // tm, n // tn),
        in_specs=[pl.BlockSpec((tm, d), lambda i, j: (i, 0)),
                  pl.BlockSpec((1, d), lambda i, j: (0, 0)),
                  pl.BlockSpec((d, tn), lambda i, j: (0, j))],
        out_specs=pl.BlockSpec((tm, tn), lambda i, j: (i, j)),
        scratch_shapes=[pltpu.VMEM((tm, d), BF16)],
        compiler_params=_cparams(("parallel", "arbitrary")),
        name="norm_matmul",
    )(x, gain.reshape(1, d), w)


def _matmul_res_kernel(*refs, n_in):
    a_refs = refs[:n_in]
    w_refs = refs[n_in:2 * n_in]
    res_ref = refs[2 * n_in]
    o_ref = refs[2 * n_in + 1]
    acc = res_ref[...]
    for a_ref, w_ref in zip(a_refs, w_refs):
        acc = acc + _dot(a_ref[...], w_ref[...].astype(BF16))
    o_ref[...] = acc


def matmul_res(a_list, w, res, tm, tn):
    n_in = len(a_list)
    t, k = a_list[0].shape
    n = w.shape[1]
    in_specs = [pl.BlockSpec((tm, k), lambda i, j: (i, 0)) for _ in range(n_in)]
    in_specs += [pl.BlockSpec((k, tn), functools.partial(lambda i, j, s: (s, j), s=s)) for s in range(n_in)]
    in_specs += [pl.BlockSpec((tm, tn), lambda i, j: (i, j))]
    return pl.pallas_call(
        functools.partial(_matmul_res_kernel, n_in=n_in),
        out_shape=jax.ShapeDtypeStruct((t, n), F32),
        grid=(t // tm, n // tn),
        in_specs=in_specs,
        out_specs=pl.BlockSpec((tm, tn), lambda i, j: (i, j)),
        compiler_params=_cparams(("parallel", "arbitrary")),
        name="matmul_res",
    )(*a_list, *([w] * n_in), res)


def _ssd_kernel(z_ref, xs_ref, bc_ref, dt_ref, cwx_ref, cbx_ref, cwb_ref, cbb_ref, alog_ref,
                dtb_ref, dskip_ref, ng_ref, e_ref, o_ref, state_ref, tailx_ref, tailb_ref,
                *, n_heads, group_heads, head_dim, n_state):
    @pl.when(pl.program_id(1) == 0)
    def _():
        state_ref[...] = jnp.zeros_like(state_ref)
        tailx_ref[...] = jnp.zeros_like(tailx_ref)
        tailb_ref[...] = jnp.zeros_like(tailb_ref)

    rows = xs_ref.shape[0]
    n_groups = n_heads // group_heads
    gw = group_heads * head_dim
    xs = _silu(_causal_conv(xs_ref[...].astype(F32), tailx_ref, cwx_ref, cbx_ref))
    bc = _silu(_causal_conv(bc_ref[...].astype(F32), tailb_ref, cwb_ref, cbb_ref))

    lane = lax.broadcasted_iota(jnp.int32, (rows, LANES), 1)
    dt = jnp.where(lane < n_heads, _softplus(dt_ref[...] + dtb_ref[...]), 0.0)
    la = dt * (-jnp.exp(alog_ref[...]))
    acum = _cumsum_rows(la)
    acum_t = acum.T
    a_last = acum[rows - 1:rows, :]
    e = e_ref[...]
    dt_x = _expand_heads(dt, e)
    eacum_x = _expand_heads(jnp.exp(acum), e)
    dend_x = _expand_heads(jnp.exp(a_last - acum), e)
    cdec_x = eacum_x[rows - 1:rows, :]

    xdt = xs * dt_x
    xdt_b = xdt.astype(BF16)
    xdend_b = (xdt * dend_x).astype(BF16)
    state = state_ref[...]
    state_b = state.astype(BF16)

    r_i = lax.broadcasted_iota(jnp.int32, (rows, rows), 0)
    c_i = lax.broadcasted_iota(jnp.int32, (rows, rows), 1)
    causal = r_i >= c_i
    lane_lo = lane < (LANES // 2)

    y_diag, y_off, st_new = [], [], []
    heads_per_tile = LANES // head_dim
    for g in range(n_groups):
        b_g = bc[:, g * n_state:(g + 1) * n_state].astype(BF16)
        c_g = bc[:, (n_groups + g) * n_state:(n_groups + g + 1) * n_state].astype(BF16)
        cb = _dot_nt(c_g, b_g)
        y_off.append(_dot(c_g, state_b[:, g * gw:(g + 1) * gw]))
        st_new.append(_dot_tn(b_g, xdend_b[:, g * gw:(g + 1) * gw]))
        for p in range(group_heads // heads_per_tile):
            h0 = g * group_heads + p * heads_per_tile
            ms = []
            for h in range(h0, h0 + heads_per_tile):
                seg = acum[:, h:h + 1] - acum_t[h:h + 1, :]
                dec = jnp.exp(jnp.where(causal, seg, NEG_INF))
                ms.append((cb * dec).astype(BF16))
            lhs = jnp.concatenate(ms, axis=1)
            xp = xdt_b[:, h0 * head_dim:h0 * head_dim + LANES]
            zero = jnp.zeros_like(xp)
            rhs = jnp.concatenate([jnp.where(lane_lo, xp, zero), jnp.where(lane_lo, zero, xp)], axis=0)
            y_diag.append(_dot(lhs, rhs))
    y = (jnp.concatenate(y_diag, axis=1) + jnp.concatenate(y_off, axis=1) * eacum_x
         + dskip_ref[...] * xs)
    state_ref[...] = state * cdec_x + jnp.concatenate(st_new, axis=1)
    y = y * _silu(z_ref[...].astype(F32))
    o_ref[...] = _group_rmsnorm(y, ng_ref[...], gw).astype(o_ref.dtype)


def ssd_mixer(cfg, proj, gates, cwx, cbx, cwb, cbb, a_log_p, dt_bias_p, d_skip_x, norm_g, e_mat):
    d, L = cfg.d_model, cfg.chunk
    nc = cfg.seq // L
    bcw = 2 * cfg.ssd_groups * cfg.ssd_state
    row = lambda b, c: b * nc + c
    full = lambda shape: pl.BlockSpec(shape, lambda b, c: (0, 0))
    kern = functools.partial(_ssd_kernel, n_heads=cfg.ssd_heads, group_heads=cfg.ssd_group_heads,
                             head_dim=cfg.ssd_head_dim, n_state=cfg.ssd_state)
    return pl.pallas_call(
        kern,
        out_shape=jax.ShapeDtypeStruct((cfg.tokens, d), BF16),
        grid=(cfg.batch, nc),
        in_specs=[pl.BlockSpec((L, d), lambda b, c: (row(b, c), 0)),
                  pl.BlockSpec((L, d), lambda b, c: (row(b, c), 1)),
                  pl.BlockSpec((L, bcw), lambda b, c: (row(b, c), 4 * d // bcw)),
                  pl.BlockSpec((L, LANES), lambda b, c: (row(b, c), 0)),
                  full(cwx.shape), full(cbx.shape), full(cwb.shape), full(cbb.shape),
                  full(a_log_p.shape), full(dt_bias_p.shape), full(d_skip_x.shape),
                  full(norm_g.shape), full(e_mat.shape)],
        out_specs=pl.BlockSpec((L, d), lambda b, c: (row(b, c), 0)),
        scratch_shapes=[pltpu.VMEM((cfg.ssd_state, d), F32),
                        pltpu.VMEM((SUBLANES, d), F32),
                        pltpu.VMEM((SUBLANES, bcw), F32)],
        compiler_params=_cparams(("parallel", "arbitrary")),
        name="ssd_mixer",
    )(proj, proj, proj, gates, cwx, cbx, cwb, cbb, a_log_p, dt_bias_p, d_skip_x, norm_g, e_mat)


def _mlstm_kernel(q_ref, k_ref, v_ref, og_ref, ig_ref, fg_ref, cwq_ref, cbq_ref, cwk_ref, cbk_ref,
                  bi_ref, bf_ref, ng_ref, o_ref, c_ref, n_ref, m_ref, tailq_ref, tailk_ref,
                  *, n_heads, dk, dv):
    @pl.when(pl.program_id(1) == 0)
    def _():
        c_ref[...] = jnp.zeros_like(c_ref)
        n_ref[...] = jnp.zeros_like(n_ref)
        m_ref[...] = jnp.full(m_ref.shape, NEG_INF, F32)
        tailq_ref[...] = jnp.zeros_like(tailq_ref)
        tailk_ref[...] = jnp.zeros_like(tailk_ref)

    rows = q_ref.shape[0]
    q = _silu(_causal_conv(q_ref[...].astype(F32), tailq_ref, cwq_ref, cbq_ref))
    k = _silu(_causal_conv(k_ref[...].astype(F32), tailk_ref, cwk_ref, cbk_ref)) * (dk ** -0.5)
    i_log = ig_ref[...] + bi_ref[...]
    f_log = -_softplus(-(fg_ref[...] + bf_ref[...]))
    b = _cumsum_rows(f_log)
    b_t = b.T
    i_t = i_log.T
    g_end = b[rows - 1:rows, :] - b + i_log
    r_i = lax.broadcasted_iota(jnp.int32, (rows, rows), 0)
    c_i = lax.broadcasted_iota(jnp.int32, (rows, rows), 1)
    causal = r_i >= c_i
    rep = dv // LANES

    def wide(x):
        return jnp.concatenate([x] * rep, axis=1)

    outs = []
    for h in range(n_heads):
        qh = q[:, h * dk:(h + 1) * dk]
        kh = k[:, h * dk:(h + 1) * dk]
        vh = v_ref[:, h * dv:(h + 1) * dv]
        qb = qh.astype(BF16)
        b_col = jnp.broadcast_to(b[:, h:h + 1], (rows, LANES))
        log_d = jnp.where(causal, b_col - b_t[h:h + 1, :] + i_t[h:h + 1, :], NEG_INF)
        m_intra = jnp.max(log_d, axis=-1, keepdims=True)
        m_prev = m_ref[h:h + 1, :]
        m_inter = b_col + m_prev
        m_t = jnp.maximum(m_inter, m_intra)
        a_inter = jnp.exp(m_inter - m_t)
        pw = jnp.exp(log_d - m_t) * _dot_nt(qb, kh.astype(BF16))
        c_prev = c_ref[h]
        n_prev = n_ref[h:h + 1, :]
        num = _dot(pw.astype(BF16), vh) + wide(a_inter) * _dot(qb, c_prev.astype(BF16))
        den = (jnp.sum(pw, axis=-1, keepdims=True)
               + a_inter * jnp.sum(qh * n_prev, axis=-1, keepdims=True))
        outs.append(num / wide(jnp.maximum(jnp.abs(den), jnp.exp(-m_t))))
        ge = jnp.broadcast_to(g_end[:, h:h + 1], (rows, LANES))
        m_loc = jnp.max(ge, axis=0, keepdims=True)
        kw = kh * jnp.exp(ge - m_loc)
        f_tot = b_col[rows - 1:rows, :]
        m_new = jnp.maximum(f_tot + m_prev, m_loc)
        a_prev = jnp.exp(f_tot + m_prev - m_new)
        a_loc = jnp.exp(m_loc - m_new)
        c_ref[h] = c_prev * wide(a_prev) + _dot_tn(kw.astype(BF16), vh) * wide(a_loc)
        n_ref[h:h + 1, :] = n_prev * a_prev + jnp.sum(kw, axis=0, keepdims=True) * a_loc
        m_ref[h:h + 1, :] = m_new
    hcat = jnp.concatenate(outs, axis=1) * _sigmoid(og_ref[...].astype(F32))
    o_ref[...] = _group_rmsnorm(hcat, ng_ref[...], dv).astype(o_ref.dtype)


def mlstm_mixer(cfg, proj, gates, cwq, cbq, cwk, cbk, b_i_p, b_f_p, norm_g):
    d, L = cfg.d_model, cfg.chunk
    nc = cfg.seq // L
    qw = cfg.mlstm_heads * cfg.mlstm_qk
    row = lambda b, c: b * nc + c
    full = lambda shape: pl.BlockSpec(shape, lambda b, c: (0, 0))
    kern = functools.partial(_mlstm_kernel, n_heads=cfg.mlstm_heads, dk=cfg.mlstm_qk, dv=cfg.mlstm_v)
    q_blk = (4 * d + 2 * cfg.ssd_groups * cfg.ssd_state) // qw
    return pl.pallas_call(
        kern,
        out_shape=jax.ShapeDtypeStruct((cfg.tokens, d), BF16),
        grid=(cfg.batch, nc),
        in_specs=[pl.BlockSpec((L, qw), lambda b, c: (row(b, c), q_blk)),
                  pl.BlockSpec((L, qw), lambda b, c: (row(b, c), q_blk + 1)),
                  pl.BlockSpec((L, d), lambda b, c: (row(b, c), 2)),
                  pl.BlockSpec((L, d), lambda b, c: (row(b, c), 3)),
                  pl.BlockSpec((L, LANES), lambda b, c: (row(b, c), 1)),
                  pl.BlockSpec((L, LANES), lambda b, c: (row(b, c), 2)),
                  full(cwq.shape), full(cbq.shape), full(cwk.shape), full(cbk.shape),
                  full(b_i_p.shape), full(b_f_p.shape), full(norm_g.shape)],
        out_specs=pl.BlockSpec((L, d), lambda b, c: (row(b, c), 0)),
        scratch_shapes=[pltpu.VMEM((cfg.mlstm_heads, cfg.mlstm_qk, cfg.mlstm_v), F32),
                        pltpu.VMEM((cfg.mlstm_heads, cfg.mlstm_qk), F32),
                        pltpu.VMEM((cfg.mlstm_heads, LANES), F32),
                        pltpu.VMEM((SUBLANES, qw), F32),
                        pltpu.VMEM((SUBLANES, qw), F32)],
        compiler_params=_cparams(("parallel", "arbitrary")),
        name="mlstm_mixer",
    )(proj, proj, proj, proj, gates, gates, cwq, cbq, cwk, cbk, b_i_p, b_f_p, norm_g)


def _hgrn_kernel(q_ref, f_ref, v_ref, g_ref, lbp_ref, ng_ref, o_ref,
                 st_ref, q_s, k_s, gc_s, v_s, o_s, *, n_heads, dk, sub):
    @pl.when(pl.program_id(1) == 0)
    def _():
        st_ref[...] = jnp.zeros_like(st_ref)

    rows = q_ref.shape[0]
    a0 = lbp_ref[0:1, :]
    a1 = lbp_ref[1:2, :]
    amax = jnp.maximum(a0, a1)
    e0 = jnp.exp(a0 - amax)
    e1 = jnp.exp(a1 - amax)
    p0 = e0 / (e0 + e1)
    p1 = e1 / (e0 + e1)
    lb = (p0 + p1) - p0
    f = lb + (1.0 - lb) * _sigmoid(f_ref[...])
    k_s[...] = 1.0 - f
    gc_s[...] = _cumsum_rows(jnp.log(f))
    q_s[...] = q_ref[...].astype(F32)
    v_s[...] = v_ref[...].astype(F32)
    row_sub = lax.broadcasted_iota(jnp.int32, (sub, dk), 0)

    def head(h, carry):
        off = pl.multiple_of(h * dk, dk)
        q = q_s[:, pl.ds(off, dk)]
        k = k_s[:, pl.ds(off, dk)]
        gc = gc_s[:, pl.ds(off, dk)]
        v = v_s[:, pl.ds(off, dk)]
        v_b = v.astype(BF16)
        st = st_ref[h]
        g_end = gc[rows - 1:rows, :]
        o = _dot_nt((q * jnp.exp(gc)).astype(BF16), st.astype(BF16))
        k_end = (k * jnp.exp(g_end - gc)).astype(BF16)
        st_ref[h] = st * jnp.exp(g_end) + _dot_tn(v_b, k_end)
        blocks = []
        for i in range(rows // sub):
            r0 = i * sub
            qi = q[r0:r0 + sub]
            gi = gc[r0:r0 + sub]
            ki = k[r0:r0 + sub]
            vi = v[r0:r0 + sub]
            oi = o[r0:r0 + sub]
            if i > 0:
                gref = gc[r0 - 1:r0, :]
                qt = (qi * jnp.exp(gi - gref)).astype(BF16)
                kt = (k[:r0] * jnp.exp(gref - gc[:r0])).astype(BF16)
                att = _dot_nt(qt, kt)
                oi = oi + _dot(att.astype(BF16), v_b[:r0])
            for s in range(sub):
                dec = jnp.exp(jnp.where(row_sub >= s, gi - gi[s:s + 1, :], NEG_INF))
                att_s = jnp.sum(qi * dec * ki[s:s + 1, :], axis=-1, keepdims=True)
                oi = oi + att_s * vi[s:s + 1, :]
            blocks.append(oi)
        o_s[:, pl.ds(off, dk)] = jnp.concatenate(blocks, axis=0)
        return carry

    lax.fori_loop(0, n_heads, head, 0, unroll=4)
    y = _group_rmsnorm(o_s[...], ng_ref[...], dk) * _silu(g_ref[...].astype(F32))
    o_ref[...] = y.astype(o_ref.dtype)


def hgrn_mixer(cfg, proj, f_pre, lb_param, norm_g):
    d, L = cfg.d_model, cfg.hgrn_chunk
    nc = cfg.seq // L
    row = lambda b, c: b * nc + c
    full = lambda shape: pl.BlockSpec(shape, lambda b, c: (0, 0))
    kern = functools.partial(_hgrn_kernel, n_heads=cfg.hgrn_heads, dk=cfg.hgrn_dk, sub=cfg.hgrn_sub)
    return pl.pallas_call(
        kern,
        out_shape=jax.ShapeDtypeStruct((cfg.tokens, d), BF16),
        grid=(cfg.batch, nc),
        in_specs=[pl.BlockSpec((L, d), lambda b, c: (row(b, c), 0)),
                  pl.BlockSpec((L, d), lambda b, c: (row(b, c), 0)),
                  pl.BlockSpec((L, d), lambda b, c: (row(b, c), 1)),
                  pl.BlockSpec((L, d), lambda b, c: (row(b, c), 2)),
                  full(lb_param.shape), full(norm_g.shape)],
        out_specs=pl.BlockSpec((L, d), lambda b, c: (row(b, c), 0)),
        scratch_shapes=[pltpu.VMEM((cfg.hgrn_heads, cfg.hgrn_dk, cfg.hgrn_dk), F32)]
                       + [pltpu.VMEM((L, d), F32)] * 5,
        compiler_params=_cparams(("parallel", "arbitrary")),
        name="hgrn_mixer",
    )(proj, f_pre, proj, proj, lb_param, norm_g)


def _swiglu_step(xn, wg, wu, wd):
    gate = _dot(xn, wg.astype(BF16))
    up = _dot(xn, wu.astype(BF16))
    return _dot((_silu(gate) * up).astype(BF16), wd.astype(BF16))


def _ffn_kernel(x_ref, gain_ref, wg_ref, wu_ref, wd_ref, o_ref, xn_ref):
    @pl.when(pl.program_id(1) == 0)
    def _():
        _rmsnorm_rows_to(x_ref, gain_ref, xn_ref)
        o_ref[...] = x_ref[...]

    o_ref[...] += _swiglu_step(xn_ref[...], wg_ref[...], wu_ref[...], wd_ref[...])


def swiglu_dense(x, gain, w_gate, w_up, w_down, tm, tn):
    t, d = x.shape
    f = w_gate.shape[1]
    return pl.pallas_call(
        _ffn_kernel,
        out_shape=jax.ShapeDtypeStruct((t, d), F32),
        grid=(t // tm, f // tn),
        in_specs=[pl.BlockSpec((tm, d), lambda i, j: (i, 0)),
                  pl.BlockSpec((1, d), lambda i, j: (0, 0)),
                  pl.BlockSpec((d, tn), lambda i, j: (0, j)),
                  pl.BlockSpec((d, tn), lambda i, j: (0, j)),
                  pl.BlockSpec((tn, d), lambda i, j: (j, 0))],
        out_specs=pl.BlockSpec((tm, d), lambda i, j: (i, 0)),
        scratch_shapes=[pltpu.VMEM((tm, d), BF16)],
        compiler_params=_cparams(("parallel", "arbitrary")),
        name="swiglu_dense",
    )(x, gain.reshape(1, d), w_gate, w_up, w_down)


def _moe_kernel(te_ref, na_ref, src_ref, x3_hbm, wg_ref, wu_ref, wd_ref, o3_ref,
                stage_ref, xn_ref, acc_ref, sem):
    del te_ref
    i = pl.program_id(0)
    j = pl.program_id(1)
    n_active = na_ref[0]
    active = i < n_active
    tm, d = xn_ref.shape
    s = d // LANES

    def row_copy(tok, r):
        return pltpu.make_async_copy(x3_hbm.at[pl.ds(pl.multiple_of(tok * s, s), s)],
                                     stage_ref.at[pl.ds(pl.multiple_of(r * s, s), s)], sem)

    def start_gather(tile):
        def body(r, carry):
            row_copy(src_ref[tile * tm + r], r).start()
            return carry
        lax.fori_loop(0, tm, body, 0, unroll=8)

    def wait_gather():
        def body(r, carry):
            row_copy(0, r).wait()
            return carry
        lax.fori_loop(0, tm, body, 0, unroll=8)

    @pl.when(jnp.logical_and(j == 0, i == 0))
    def _():
        start_gather(0)

    @pl.when(jnp.logical_and(j == 0, active))
    def _():
        wait_gather()
        _slabs_to_rows(stage_ref, xn_ref)
        acc_ref[...] = jnp.zeros_like(acc_ref)

    @pl.when(jnp.logical_and(j == 0, i + 1 < n_active))
    def _():
        start_gather(i + 1)

    @pl.when(active)
    def _():
        acc_ref[...] += _swiglu_step(xn_ref[...], wg_ref[0], wu_ref[0], wd_ref[0])

    @pl.when(j == pl.num_programs(1) - 1)
    def _():
        _rows_to_slabs(acc_ref, o3_ref)


def swiglu_moe(x3, src_tok, w_gate, w_up, w_down, tile_expert, n_active, n_tiles, tm, tn):
    d = w_gate.shape[1]
    s = d // LANES
    f = w_gate.shape[2]
    grid_spec = pltpu.PrefetchScalarGridSpec(
        num_scalar_prefetch=3,
        grid=(n_tiles, f // tn),
        in_specs=[pl.BlockSpec(memory_space=pl.ANY),
                  pl.BlockSpec((1, d, tn), lambda i, j, te, na, src: (te[i], 0, j)),
                  pl.BlockSpec((1, d, tn), lambda i, j, te, na, src: (te[i], 0, j)),
                  pl.BlockSpec((1, tn, d), lambda i, j, te, na, src: (te[i], j, 0))],
        out_specs=pl.BlockSpec((tm * s, LANES), lambda i, j, te, na, src: (i, 0)),
        scratch_shapes=[pltpu.VMEM((tm * s, LANES), F32),
                        pltpu.VMEM((tm, d), BF16),
                        pltpu.VMEM((tm, d), F32),
                        pltpu.SemaphoreType.DMA(())],
    )
    return pl.pallas_call(
        _moe_kernel,
        out_shape=jax.ShapeDtypeStruct((n_tiles * tm * s, LANES), F32),
        grid_spec=grid_spec,
        compiler_params=_cparams(("arbitrary", "arbitrary")),
        name="swiglu_moe",
    )(tile_expert, n_active, src_tok, x3, w_gate, w_up, w_down)


def _router_kernel(x_ref, gain_ref, wr_ref, x3_ref, route_ref, count_ref, xn_ref, run_ref, tri_ref,
                   *, n_experts):
    i = pl.program_id(0)
    rows = x_ref.shape[0]

    @pl.when(i == 0)
    def _():
        run_ref[...] = jnp.zeros_like(run_ref)
        r_i = lax.broadcasted_iota(jnp.int32, (rows, rows), 0)
        c_i = lax.broadcasted_iota(jnp.int32, (rows, rows), 1)
        tri_ref[...] = jnp.where(r_i > c_i, 1.0, 0.0).astype(BF16)

    _rmsnorm_rows_to(x_ref, gain_ref, xn_ref)
    _rows_to_slabs(xn_ref, x3_ref)
    xn = xn_ref[...]
    x_hi = xn.astype(BF16)
    x_lo = (xn - x_hi.astype(F32)).astype(BF16)
    w = wr_ref[...]
    w_hi = w.astype(BF16)
    w_lo = (w - w_hi.astype(F32)).astype(BF16)
    logits = _dot(x_hi, w_hi) + (_dot(x_lo, w_hi) + _dot(x_hi, w_lo))

    lane = lax.broadcasted_iota(jnp.int32, logits.shape, 1)
    lane_f = lane.astype(F32)
    big = float(LANES)
    l1 = jnp.where(lane < n_experts, logits, NEG_INF)
    m1 = jnp.max(l1, axis=-1, keepdims=True)
    idx1 = jnp.min(jnp.where(l1 == m1, lane_f, big), axis=-1, keepdims=True)
    hot1 = lane_f == idx1
    l2 = jnp.where(hot1, NEG_INF, l1)
    m2 = jnp.max(l2, axis=-1, keepdims=True)
    idx2 = jnp.min(jnp.where(l2 == m2, lane_f, big), axis=-1, keepdims=True)
    hot2 = lane_f == idx2
    e2 = jnp.exp(m2 - m1)
    g1 = 1.0 / (1.0 + e2)
    g2 = e2 / (1.0 + e2)

    cnt = jnp.where(hot1, 1.0, 0.0) + jnp.where(hot2, 1.0, 0.0)
    before = _dot(tri_ref[...], cnt.astype(BF16)) + run_ref[...]
    rank1 = jnp.sum(jnp.where(hot1, before, 0.0), axis=-1, keepdims=True)
    rank2 = jnp.sum(jnp.where(hot2, before, 0.0), axis=-1, keepdims=True)
    run_ref[...] = run_ref[...] + jnp.sum(cnt, axis=0, keepdims=True)
    count_ref[...] = jnp.broadcast_to(run_ref[...], count_ref.shape)

    out = jnp.zeros(logits.shape, F32)
    for pos, val in enumerate((idx1, idx2, rank1, rank2, g1, g2)):
        out = jnp.where(lane == pos, val, out)
    route_ref[...] = out


def router(x, gain, w_router_p, n_experts, tm):
    t, d = x.shape
    return pl.pallas_call(
        functools.partial(_router_kernel, n_experts=n_experts),
        out_shape=(jax.ShapeDtypeStruct((t * (d // LANES), LANES), F32),
                   jax.ShapeDtypeStruct((t, LANES), F32),
                   jax.ShapeDtypeStruct((SUBLANES, LANES), F32)),
        grid=(t // tm,),
        in_specs=[pl.BlockSpec((tm, d), lambda i: (i, 0)),
                  pl.BlockSpec((1, d), lambda i: (0, 0)),
                  pl.BlockSpec((d, LANES), lambda i: (0, 0))],
        out_specs=(pl.BlockSpec((tm * (d // LANES), LANES), lambda i: (i, 0)),
                   pl.BlockSpec((tm, LANES), lambda i: (i, 0)),
                   pl.BlockSpec((SUBLANES, LANES), lambda i: (0, 0))),
        scratch_shapes=[pltpu.VMEM((tm, d), F32), pltpu.VMEM((1, LANES), F32), pltpu.VMEM((tm, tm), BF16)],
        compiler_params=_cparams(("arbitrary",)),
        name="router",
    )(x, gain.reshape(1, d), w_router_p)


def _combine_kernel(dest_ref, h_ref, y3_hbm, route_ref, gain_ref, o_ref, stage_ref, y_ref, sems, *, fanout):
    i = pl.program_id(0)
    n_steps = pl.num_programs(0)
    tc, d = h_ref.shape
    ns = d // LANES
    slot = lax.rem(i, 2)

    def row_copy(row, r, sl):
        return pltpu.make_async_copy(y3_hbm.at[pl.ds(pl.multiple_of(row * ns, ns), ns)],
                                     stage_ref.at[sl, pl.ds(pl.multiple_of(r * ns, ns), ns)], sems.at[sl])

    def start_gather(step, sl):
        def body(r, carry):
            for s in range(fanout):
                row_copy(dest_ref[(step * tc + r) * fanout + s], s * tc + r, sl).start()
            return carry
        lax.fori_loop(0, tc, body, 0, unroll=8)

    def wait_gather(sl):
        def body(r, carry):
            row_copy(0, r, sl).wait()
            return carry
        lax.fori_loop(0, fanout * tc, body, 0, unroll=8)

    @pl.when(i == 0)
    def _():
        start_gather(0, 0)

    @pl.when(i + 1 < n_steps)
    def _():
        start_gather(i + 1, 1 - slot)

    wait_gather(slot)
    route = route_ref[...]
    acc = h_ref[...]
    for s in range(fanout):
        _slabs_to_rows(stage_ref.at[slot, pl.ds(s * tc * ns, tc * ns)], y_ref)
        acc = acc + route[:, 4 + s:5 + s] * y_ref[...]
    ms = jnp.mean(acc * acc, axis=-1, keepdims=True)
    o_ref[...] = acc * lax.rsqrt(ms + NORM_EPS) * gain_ref[...]


def combine_norm(h, y3, dest, route, gain, tc, fanout):
    t, d = h.shape
    s = d // LANES
    grid_spec = pltpu.PrefetchScalarGridSpec(
        num_scalar_prefetch=1,
        grid=(t // tc,),
        in_specs=[pl.BlockSpec((tc, d), lambda i, dest: (i, 0)),
                  pl.BlockSpec(memory_space=pl.ANY),
                  pl.BlockSpec((tc, LANES), lambda i, dest: (i, 0)),
                  pl.BlockSpec((1, d), lambda i, dest: (0, 0))],
        out_specs=pl.BlockSpec((tc, d), lambda i, dest: (i, 0)),
        scratch_shapes=[pltpu.VMEM((2, fanout * tc * s, LANES), F32),
                        pltpu.VMEM((tc, d), F32),
                        pltpu.SemaphoreType.DMA((2,))],
    )
    return pl.pallas_call(
        functools.partial(_combine_kernel, fanout=fanout),
        out_shape=jax.ShapeDtypeStruct((t, d), F32),
        grid_spec=grid_spec,
        compiler_params=_cparams(("arbitrary",)),
        name="combine_norm",
    )(dest, h, y3, route, gain.reshape(1, d))


def _pad_lanes(v, offset=0, width=LANES):
    return jnp.zeros((1, width), F32).at[0, offset:offset + v.shape[0]].set(v.astype(F32))


def forward(cfg, x, ev_norm_mix, ev_w_in, ev_ssd_conv_w, ev_ssd_conv_b, ev_ssd_a_log, ev_ssd_dt_bias,
            ev_ssd_d, ev_ssd_norm, ev_mlstm_conv_w, ev_mlstm_conv_b, ev_mlstm_b_i, ev_mlstm_b_f,
            ev_mlstm_norm, ev_w_out, ev_norm_ffn, ev_ffn_gate, ev_ffn_up, ev_ffn_down,
            od_norm_mix, od_w_in, od_hgrn_norm, od_w_out, od_norm_ffn, od_router,
            od_exp_gate, od_exp_up, od_exp_down, hgrn_lb_param, final_norm):
    d, t = cfg.d_model, cfg.tokens
    tm = min(cfg.tm, t)
    h = x.reshape(t, d)

    nh, mh = cfg.ssd_heads, cfg.mlstm_heads
    bcw = 2 * cfg.ssd_groups * cfg.ssd_state
    qw = mh * cfg.mlstm_qk
    sizes = (d, d + bcw, nh, qw, qw, mh * cfg.mlstm_v, mh * cfg.mlstm_v, mh, mh)
    cuts = [0]
    for s in sizes:
        cuts.append(cuts[-1] + s)
    w_in = ev_w_in[0]
    col = lambda a, b: w_in[:, a:b]
    z0, xbc0, dt0, q0, k0, v0, o0, i0, f0 = cuts[:9]
    w_main = jnp.concatenate([col(z0, z0 + d), col(xbc0, xbc0 + d), col(v0, o0), col(o0, i0),
                              col(xbc0 + d, dt0), col(q0, k0), col(k0, v0)], axis=1).astype(BF16)
    w_gate = jnp.zeros((d, 3 * LANES), F32)
    w_gate = w_gate.at[:, 0:nh].set(col(dt0, q0))
    w_gate = w_gate.at[:, LANES:LANES + mh].set(col(i0, f0))
    w_gate = w_gate.at[:, 2 * LANES:2 * LANES + mh].set(col(f0, cuts[9]))

    tn_proj = min(cfg.tn_proj, w_main.shape[1])
    while w_main.shape[1] % tn_proj:
        tn_proj //= 2
    proj = norm_matmul(h, ev_norm_mix[0], w_main, BF16, tm, tn_proj)
    gates = norm_matmul(h, ev_norm_mix[0], w_gate.astype(BF16), F32, tm, 3 * LANES)

    cw, cb = ev_ssd_conv_w[0], ev_ssd_conv_b[0]
    e_mat = (jnp.arange(LANES)[:, None] == (jnp.arange(d)[None, :] // cfg.ssd_head_dim)).astype(BF16)
    y_a = ssd_mixer(cfg, proj, gates, cw[:, :d], cb[None, :d], cw[:, d:], cb[None, d:],
                    _pad_lanes(ev_ssd_a_log[0]), _pad_lanes(ev_ssd_dt_bias[0]),
                    jnp.repeat(ev_ssd_d[0].astype(F32), cfg.ssd_head_dim)[None, :],
                    ev_ssd_norm[0][None, :], e_mat)
    mw, mb = ev_mlstm_conv_w[0], ev_mlstm_conv_b[0]
    y_b = mlstm_mixer(cfg, proj, gates, mw[:, :qw], mb[None, :qw], mw[:, qw:], mb[None, qw:],
                      _pad_lanes(ev_mlstm_b_i[0]), _pad_lanes(ev_mlstm_b_f[0]), ev_mlstm_norm[0][None, :])
    tn_out = min(cfg.tn_proj, d)
    h = matmul_res([y_a, y_b], ev_w_out[0].astype(BF16), h, tm, tn_out)

    h = swiglu_dense(h, ev_norm_ffn[0], ev_ffn_gate[0].astype(BF16), ev_ffn_up[0].astype(BF16),
                     ev_ffn_down[0].astype(BF16), tm, min(cfg.tn_ffn, cfg.d_ff_dense))

    w2 = od_w_in[0]
    w_qig = jnp.concatenate([w2[:, 0:d], w2[:, 2 * d:3 * d], w2[:, 3 * d:4 * d]], axis=1).astype(BF16)
    proj2 = norm_matmul(h, od_norm_mix[0], w_qig, BF16, tm, tn_out)
    f_pre = norm_matmul(h, od_norm_mix[0], w2[:, d:2 * d].astype(BF16), F32, tm, tn_out)
    y_c = hgrn_mixer(cfg, proj2, f_pre, hgrn_lb_param.astype(F32), od_hgrn_norm[0][None, :])
    h = matmul_res([y_c], od_w_out[0].astype(BF16), h, tm, tn_out)

    ne = cfg.n_experts
    w_router_p = jnp.zeros((d, LANES), F32).at[:, :ne].set(od_router[0])
    x3, route, counts = router(h, od_norm_ffn[0], w_router_p, ne, tm)
    counts = counts[0, :ne].astype(jnp.int32)
    padded = ((counts + tm - 1) // tm) * tm
    ends = jnp.cumsum(padded)
    starts = ends - padded
    n_tiles_moe = (cfg.top_k * t) // tm + ne
    n_rows = n_tiles_moe * tm
    idx = route[:, 0:2].astype(jnp.int32)
    rank = route[:, 2:4].astype(jnp.int32)
    dest = (starts[idx] + rank).reshape(-1)
    tile_expert = jnp.minimum(
        jnp.sum((jnp.arange(n_tiles_moe)[:, None] * tm) >= ends[None, :], axis=1), ne - 1).astype(jnp.int32)
    n_active = (ends[ne - 1:ne] // tm).astype(jnp.int32)
    src_tok = jnp.zeros((n_rows,), jnp.int32).at[dest].set(jnp.arange(cfg.top_k * t, dtype=jnp.int32) // cfg.top_k)
    y3 = swiglu_moe(x3, src_tok, od_exp_gate[0], od_exp_up[0], od_exp_down[0],
                    tile_expert, n_active, n_tiles_moe, tm, min(cfg.tn_moe, cfg.d_ff_expert))
    out = combine_norm(h, y3, dest, route, final_norm, min(cfg.rows_per_combine_step, t), cfg.top_k)
    return out.reshape(cfg.batch, cfg.seq, d)


def kernel(x, ev_norm_mix, ev_w_in, ev_ssd_conv_w, ev_ssd_conv_b, ev_ssd_a_log, ev_ssd_dt_bias, ev_ssd_d, ev_ssd_norm, ev_mlstm_conv_w, ev_mlstm_conv_b, ev_mlstm_b_i, ev_mlstm_b_f, ev_mlstm_norm, ev_w_out, ev_norm_ffn, ev_ffn_gate, ev_ffn_up, ev_ffn_down, od_norm_mix, od_w_in, od_hgrn_norm, od_w_out, od_norm_ffn, od_router, od_exp_gate, od_exp_up, od_exp_down, hgrn_lb_param, final_norm):
    return forward(Cfg(), x, ev_norm_mix, ev_w_in, ev_ssd_conv_w, ev_ssd_conv_b, ev_ssd_a_log, ev_ssd_dt_bias,
                   ev_ssd_d, ev_ssd_norm, ev_mlstm_conv_w, ev_mlstm_conv_b, ev_mlstm_b_i, ev_mlstm_b_f,
                   ev_mlstm_norm, ev_w_out, ev_norm_ffn, ev_ffn_gate, ev_ffn_up, ev_ffn_down,
                   od_norm_mix, od_w_in, od_hgrn_norm, od_w_out, od_norm_ffn, od_router,
                   od_exp_gate, od_exp_up, od_exp_down, hgrn_lb_param, final_norm)
```
